```python
import math
import jax, jax.numpy as jnp
from jax import lax
import numpy as np

D_MODEL = 1024
BATCH = 8
SEQ = 2048
DEPTH = 2

GRID_W = 64
CTX_LEN = 256
QBLOCK = 128
ROPE_BASE = 10000.0
EPS = 1e-6
RNN_WIDTH = D_MODEL
RNN_BLOCKS = 8
RNN_BLOCK_W = RNN_WIDTH // RNN_BLOCKS
CONV_W = 4
LRU_C = 8.0
MLA_HEADS = 16
MLA_Q_RANK = 3 * D_MODEL // 8
MLA_KV_RANK = D_MODEL // 4
MLA_NOPE = 64
MLA_ROPE = 32
MLA_V = 64
MLA_SCALE = (MLA_NOPE + MLA_ROPE) ** -0.5
DIFF_HEADS = 8
DIFF_HD = 64
DIFF_V = 2 * DIFF_HD
DIFF_SCALE = DIFF_HD ** -0.5
FFN_HIDDEN = -(-8 * D_MODEL // (3 * 256)) * 256
N_BRANCH = 3
IN_SPLITS = (RNN_WIDTH, RNN_WIDTH, MLA_Q_RANK, MLA_KV_RANK, MLA_ROPE,
             DIFF_HEADS * 2 * DIFF_HD, DIFF_HEADS * 2 * DIFF_HD, DIFF_HEADS * DIFF_V,
             N_BRANCH * D_MODEL)
IN_COLS = sum(IN_SPLITS)

kernel_name = "hybrid_rglru_mla_diffattn_dit_prefix"


def _rmsnorm(x, g):
    xf = x.astype(jnp.float32)
    y = xf * lax.rsqrt(jnp.mean(xf * xf, axis=-1, keepdims=True) + EPS)
    return (y * g.astype(jnp.float32)).astype(x.dtype)


def _split_cols(p, sizes):
    idx = np.cumsum(sizes)[:-1].tolist()
    return jnp.split(p, idx, axis=-1)


def _axial_rope_tables(T, rot_dim):
    rows = T // GRID_W
    row_ids = jnp.repeat(jnp.arange(rows, dtype=jnp.float32), GRID_W)
    col_ids = jnp.tile(jnp.arange(GRID_W, dtype=jnp.float32), rows)
    n = rot_dim // 4
    freqs = ROPE_BASE ** (-jnp.arange(n, dtype=jnp.float32) / n)
    ang = jnp.concatenate([row_ids[:, None] * freqs, col_ids[:, None] * freqs], axis=-1)
    return jnp.cos(ang), jnp.sin(ang)


def _apply_rope(x, cos, sin):
    shp = x.shape
    xf = x.astype(jnp.float32).reshape(shp[:-1] + (shp[-1] // 2, 2))
    bshape = (1, cos.shape[0]) + (1,) * (x.ndim - 3) + (cos.shape[1],)
    c = cos.reshape(bshape)
    s = sin.reshape(bshape)
    x1, x2 = xf[..., 0], xf[..., 1]
    out = jnp.stack([x1 * c - x2 * s, x1 * s + x2 * c], axis=-1)
    return out.reshape(shp).astype(x.dtype)


def _attention(q, k, v, mix_w, scale):
    B, Tq, M, H, dk = q.shape
    nb = Tq // QBLOCK
    qb = jnp.moveaxis(q.reshape(B, nb, QBLOCK, M, H, dk), 1, 0)
    w = mix_w.astype(jnp.float32)

    def block(qi):
        s = jnp.einsum('bqmhd,bkmhd->bmhqk', qi, k).astype(jnp.float32) * scale
        p = jnp.einsum('m,bmhqk->bhqk', w, jax.nn.softmax(s, axis=-1))
        return jnp.einsum('bhqk,bkhd->bqhd', p.astype(v.dtype), v)

    o = lax.map(block, qb)
    return jnp.moveaxis(o, 0, 1).reshape(B, Tq, H, v.shape[-1])


def _centred_dwconv(x, w, b):
    T = x.shape[1]
    left = CONV_W // 2
    xp = jnp.pad(x, ((0, 0), (left, CONV_W - 1 - left), (0, 0)))
    y = b + xp[:, 0:T] * w[0]
    for j in range(1, CONV_W):
        y = y + xp[:, j:j + T] * w[j]
    return y


def _lru_coeffs(x, wa, ba, wi, bi, lam):
    B, T, W = x.shape
    xb = x.reshape(B, T, RNN_BLOCKS, RNN_BLOCK_W)
    gr = jnp.einsum('btnc,rncd->rbtnd', xb, wa).reshape(2, B, T, W) + ba[:, None, None, :]
    gi = jnp.einsum('btnc,rncd->rbtnd', xb, wi).reshape(2, B, T, W) + bi[:, None, None, :]
    r = jax.nn.sigmoid(gr.astype(jnp.float32))
    i = jax.nn.sigmoid(gi.astype(jnp.float32))
    log_a = -LRU_C * r * jax.nn.softplus(-lam.astype(jnp.float32))[:, None, None, :]
    a = jnp.exp(log_a)
    b = jnp.sqrt(-jnp.expm1(2.0 * log_a)) * i * x.astype(jnp.float32)
    return a, b


def _scan_combine(e1, e2):
    a1, b1 = e1
    a2, b2 = e2
    return a1 * a2, a2 * b1 + b2


def _linear_scan(a, b, reverse, h0=None):
    a_cum, h = lax.associative_scan(_scan_combine, (a, b), reverse=reverse, axis=1)
    if h0 is not None:
        h = h + a_cum * h0[:, None, :]
    return h


def _mla_q(cq, lp, rope):
    B, T, _ = cq.shape
    q = (_rmsnorm(cq, lp['mla_qn_g']) @ lp['mla_w_uq']).reshape(B, T, MLA_HEADS, MLA_NOPE + MLA_ROPE)
    q = _rmsnorm(q, lp['mla_q_g'])
    if rope is not None:
        q = jnp.concatenate([q[..., :MLA_NOPE], _apply_rope(q[..., MLA_NOPE:], *rope)], axis=-1)
    return q[:, :, None]


def _mla_kv(ckv, kr, lp, rope):
    B, T, _ = ckv.shape
    kv = (_rmsnorm(ckv, lp['mla_kvn_g']) @ lp['mla_w_ukv']).reshape(B, T, MLA_HEADS, MLA_NOPE + MLA_V)
    k = jnp.concatenate([kv[..., :MLA_NOPE],
                         jnp.broadcast_to(kr[:, :, None, :], (B, T, MLA_HEADS, MLA_ROPE))], axis=-1)
    k = _rmsnorm(k, lp['mla_k_g'])
    if rope is not None:
        k = jnp.concatenate([k[..., :MLA_NOPE], _apply_rope(k[..., MLA_NOPE:], *rope)], axis=-1)
    return k[:, :, None], kv[..., MLA_NOPE:]


def _diff_qk(t, g, rope):
    B, T, _ = t.shape
    u = jnp.moveaxis(t.reshape(B, T, DIFF_HEADS, 2, DIFF_HD), 3, 2)
    u = _rmsnorm(u, g)
    if rope is not None:
        u = _apply_rope(u, *rope)
    return u


def _merge(y_a, rg, o_b, o_c, mg, lp, lam_init):
    B, T, _ = rg.shape
    br_a = (y_a.astype(rg.dtype) * jax.nn.gelu(rg)) @ lp['w_br_a']
    br_b = o_b.reshape(B, T, MLA_HEADS * MLA_V) @ lp['w_br_b']
    oc = _rmsnorm(o_c, lp['diff_subln_g']) * (1.0 - lam_init)
    br_c = oc.reshape(B, T, DIFF_HEADS * DIFF_V) @ lp['w_br_c']
    ga, gb, gc = jnp.split(jax.nn.sigmoid(mg), N_BRANCH, axis=-1)
    return (ga * br_a + gb * br_b + gc * br_c) @ lp['w_out']


def _ffn(h, lp):
    gate, up = jnp.split(h @ lp['w_ffn_in'], 2, axis=-1)
    return (jax.nn.silu(gate) * up) @ lp['w_ffn_out']


def _layer(x, xc, mod, mod_c, lp, rope_mla, rope_diff, lam_init, need_ctx):
    sh1, sc1, g1, sh2, sc2, g2 = jnp.split(mod, 6, axis=-1)
    csh1, csc1, cg1, csh2, csc2, cg2 = jnp.split(mod_c, 6, axis=-1)
    h = _rmsnorm(x, lp['norm1_g']) * (1.0 + sc1) + sh1
    hc = _rmsnorm(xc, lp['norm1_g']) * (1.0 + csc1) + csh1
    rx, rg, cq, ckv, kr, dq, dk, dv, mg = _split_cols(h @ lp['w_in'], IN_SPLITS)
    rxc, rgc, cqc, ckvc, krc, dqc, dkc, dvc, mgc = _split_cols(hc @ lp['w_in'], IN_SPLITS)
    B, T, _ = x.shape
    C = xc.shape[1]

    a, b = _lru_coeffs(_centred_dwconv(rx, lp['conv_w'], lp['conv_b']),
                       lp['lru_wa'], lp['lru_ba'], lp['lru_wi'], lp['lru_bi'], lp['lru_lambda'])
    ac, bc = _lru_coeffs(_centred_dwconv(rxc, lp['conv_w'], lp['conv_b']),
                         lp['lru_wa'], lp['lru_ba'], lp['lru_wi'], lp['lru_bi'], lp['lru_lambda'])
    hcf = _linear_scan(ac[0], bc[0], False)
    hcb = _linear_scan(ac[1], bc[1], True)
    y_a = _linear_scan(a[0], b[0], False, hcf[:, -1]) + _linear_scan(a[1], b[1], True, hcb[:, 0])

    k_l, v_l = _mla_kv(ckv, kr, lp, rope_mla)
    k_c, v_c = _mla_kv(ckvc, krc, lp, None)
    ones1 = jnp.ones((1,), jnp.float32)
    o_b = _attention(_mla_q(cq, lp, rope_mla), jnp.concatenate([k_l, k_c], axis=1),
                     jnp.concatenate([v_l, v_c], axis=1), ones1, MLA_SCALE)

    dl = lp['diff_lambda'].astype(jnp.float32)
    lam = jnp.exp(jnp.sum(dl[0] * dl[1])) - jnp.exp(jnp.sum(dl[2] * dl[3])) + lam_init
    wc = jnp.stack([jnp.ones((), jnp.float32), -lam])
    kd_l = _diff_qk(dk, lp['diff_k_g'], rope_diff)
    kd_c = _diff_qk(dkc, lp['diff_k_g'], None)
    vd_l = dv.reshape(B, T, DIFF_HEADS, DIFF_V)
    vd_c = dvc.reshape(B, C, DIFF_HEADS, DIFF_V)
    o_c = _attention(_diff_qk(dq, lp['diff_q_g'], rope_diff), jnp.concatenate([kd_l, kd_c], axis=1),
                     jnp.concatenate([vd_l, vd_c], axis=1), wc, DIFF_SCALE)

    x = x + g1 * _merge(y_a, rg, o_b, o_c, mg, lp, lam_init)
    x = x + g2 * _ffn(_rmsnorm(x, lp['norm2_g']) * (1.0 + sc2) + sh2, lp)

    if need_ctx:
        o_bc = _attention(_mla_q(cqc, lp, None), k_c, v_c, ones1, MLA_SCALE)
        o_cc = _attention(_diff_qk(dqc, lp['diff_q_g'], None), kd_c, vd_c, wc, DIFF_SCALE)
        xc = xc + cg1 * _merge(hcf + hcb, rgc, o_bc, o_cc, mgc, lp, lam_init)
        xc = xc + cg2 * _ffn(_rmsnorm(xc, lp['norm2_g']) * (1.0 + csc2) + csh2, lp)
    return x, xc


def setup_inputs(seed: int = 0) -> dict:
    key = jax.random.key(seed)
    ks = iter(jax.random.split(key, 40))
    L = DEPTH
    f32 = jnp.float32

    def nrm(shape, fan_in):
        return jax.random.normal(next(ks), shape, f32) * fan_in ** -0.5

    def gain(shape):
        return 1.0 + 0.02 * jax.random.normal(next(ks), shape, f32)

    def small(shape):
        return 0.01 * jax.random.normal(next(ks), shape, f32)

    u = jax.random.uniform(next(ks), (L, 2, RNN_WIDTH), f32, 0.9, 0.999)
    a0 = u ** (1.0 / LRU_C)
    lru_lambda = jnp.log(a0) - jnp.log1p(-a0)
    return {
        'x': jax.random.normal(next(ks), (BATCH, SEQ, D_MODEL), f32),
        'c': jax.random.normal(next(ks), (BATCH, D_MODEL), f32),
        'ctx': jax.random.normal(next(ks), (BATCH, CTX_LEN, D_MODEL), f32),
        'c_ctx': jax.random.normal(next(ks), (D_MODEL,), f32),
        'w_mod': nrm((L, D_MODEL, 6 * D_MODEL), D_MODEL),
        'b_mod': small((L, 6 * D_MODEL)),
        'norm1_g': gain((L, D_MODEL)),
        'norm2_g': gain((L, D_MODEL)),
        'w_in': nrm((L, D_MODEL, IN_COLS), D_MODEL),
        'conv_w': nrm((L, CONV_W, RNN_WIDTH), CONV_W),
        'conv_b': small((L, RNN_WIDTH)),
        'lru_wa': nrm((L, 2, RNN_BLOCKS, RNN_BLOCK_W, RNN_BLOCK_W), RNN_BLOCK_W),
        'lru_ba': small((L, 2, RNN_WIDTH)),
        'lru_wi': nrm((L, 2, RNN_BLOCKS, RNN_BLOCK_W, RNN_BLOCK_W), RNN_BLOCK_W),
        'lru_bi': small((L, 2, RNN_WIDTH)),
        'lru_lambda': lru_lambda,
        'mla_qn_g': gain((L, MLA_Q_RANK)),
        'mla_w_uq': nrm((L, MLA_Q_RANK, MLA_HEADS * (MLA_NOPE + MLA_ROPE)), MLA_Q_RANK),
        'mla_kvn_g': gain((L, MLA_KV_RANK)),
        'mla_w_ukv': nrm((L, MLA_KV_RANK, MLA_HEADS * (MLA_NOPE + MLA_V)), MLA_KV_RANK),
        'mla_q_g': gain((L, MLA_NOPE + MLA_ROPE)),
        'mla_k_g': gain((L, MLA_NOPE + MLA_ROPE)),
        'diff_q_g': gain((L, DIFF_HD)),
        'diff_k_g': gain((L, DIFF_HD)),
        'diff_lambda': 0.1 * jax.random.normal(next(ks), (L, 4, DIFF_HD), f32),
        'diff_subln_g': gain((L, DIFF_V)),
        'w_br_a': nrm((L, RNN_WIDTH, D_MODEL), RNN_WIDTH),
        'w_br_b': nrm((L, MLA_HEADS * MLA_V, D_MODEL), MLA_HEADS * MLA_V),
        'w_br_c': nrm((L, DIFF_HEADS * DIFF_V, D_MODEL), DIFF_HEADS * DIFF_V),
        'w_out': nrm((L, D_MODEL, D_MODEL), D_MODEL),
        'w_ffn_in': nrm((L, D_MODEL, 2 * FFN_HIDDEN), D_MODEL),
        'w_ffn_out': nrm((L, FFN_HIDDEN, D_MODEL), FFN_HIDDEN),
    }


def reference(x, c, ctx, c_ctx, w_mod, b_mod, norm1_g, norm2_g, w_in, conv_w, conv_b,
              lru_wa, lru_ba, lru_wi, lru_bi, lru_lambda, mla_qn_g, mla_w_uq, mla_kvn_g,
              mla_w_ukv, mla_q_g, mla_k_g, diff_q_g, diff_k_g, diff_lambda, diff_subln_g,
              w_br_a, w_br_b, w_br_c, w_out, w_ffn_in, w_ffn_out):
    T = x.shape[1]
    rope_mla = _axial_rope_tables(T, MLA_ROPE)
    rope_diff = _axial_rope_tables(T, DIFF_HD)
    xc = ctx
    sc = jax.nn.silu(c)
    scc = jax.nn.silu(c_ctx)
    for l in range(DEPTH):
        lp = dict(norm1_g=norm1_g[l], norm2_g=norm2_g[l], w_in=w_in[l], conv_w=conv_w[l],
                  conv_b=conv_b[l], lru_wa=lru_wa[l], lru_ba=lru_ba[l], lru_wi=lru_wi[l],
                  lru_bi=lru_bi[l], lru_lambda=lru_lambda[l], mla_qn_g=mla_qn_g[l],
                  mla_w_uq=mla_w_uq[l], mla_kvn_g=mla_kvn_g[l], mla_w_ukv=mla_w_ukv[l],
                  mla_q_g=mla_q_g[l], mla_k_g=mla_k_g[l], diff_q_g=diff_q_g[l],
                  diff_k_g=diff_k_g[l], diff_lambda=diff_lambda[l], diff_subln_g=diff_subln_g[l],
                  w_br_a=w_br_a[l], w_br_b=w_br_b[l], w_br_c=w_br_c[l], w_out=w_out[l],
                  w_ffn_in=w_ffn_in[l], w_ffn_out=w_ffn_out[l])
        mod = (sc @ w_mod[l] + b_mod[l])[:, None, :]
        mod_c = scc @ w_mod[l] + b_mod[l]
        lam_init = 0.8 - 0.6 * math.exp(-0.3 * l)
        x, xc = _layer(x, xc, mod, mod_c, lp, rope_mla, rope_diff, lam_init, l < DEPTH - 1)
    return x
```

```python
import functools
import math

import numpy as np
import jax
import jax.numpy as jnp
from jax import lax
from jax.experimental import pallas as pl
from jax.experimental.pallas import tpu as pltpu

F32 = jnp.float32
BF16 = jnp.bfloat16

EPS = 1e-6
GRID_W = 64
ROPE_BASE = 10000.0
LRU_C = 8.0
CONV_W = 4
RNN_BLOCKS = 8
MLA_HEADS = 16
MLA_NOPE = 64
MLA_ROPE = 32
MLA_V = 64
MLA_SCALE = (MLA_NOPE + MLA_ROPE) ** -0.5
DIFF_HEADS = 8
DIFF_HD = 64
DIFF_V = 2 * DIFF_HD
DIFF_SCALE = DIFF_HD ** -0.5
N_BRANCH = 3

LANES = 128
SUBLANES = 8
BF16_ROWS = 16
VMEM_LIMIT = 56 * 1024 * 1024

COL_MLA, COL_RX, COL_RG, COL_DQ, COL_DK, COL_DV, COL_MG = 0, 1, 2, 3, 4, 5, 6


def _cparams(n_axes):
    return pltpu.CompilerParams(dimension_semantics=("arbitrary",) * n_axes,
                                vmem_limit_bytes=VMEM_LIMIT)


def _sigmoid(x):
    return 1.0 / (1.0 + jnp.exp(-x))


def _silu(x):
    return x * _sigmoid(x)


def _gelu_tanh(x):
    return 0.5 * x * (1.0 + jnp.tanh(math.sqrt(2.0 / math.pi) * (x + 0.044715 * (x * x * x))))


def _rms(x, denom):
    ms = jnp.sum(x * x, axis=-1, keepdims=True) * (1.0 / denom)
    return x * lax.rsqrt(ms + EPS)


def _modulate(xn, scale_ref, shift_ref):
    tm, d = xn.shape
    x3 = xn.reshape(tm // SUBLANES, SUBLANES, d)
    y = x3 * (1.0 + scale_ref[...])[None] + shift_ref[...][None]
    return y.reshape(tm, d)


def _gate_rows(v, gate_ref):
    tm, d = v.shape
    return (v.reshape(tm // SUBLANES, SUBLANES, d) * gate_ref[...][None]).reshape(tm, d)


def _mod_kernel(c_ref, w_ref, b_ref, o_ref):
    s = _silu(c_ref[...]).astype(BF16)
    o_ref[...] = jnp.dot(s, w_ref[...].astype(BF16), preferred_element_type=F32) + b_ref[...]


def _mod_call(cc, w_mod, b_mod):
    depth, d, n = w_mod.shape
    tn = 1024
    return pl.pallas_call(
        _mod_kernel,
        grid=(depth, n // tn),
        in_specs=[pl.BlockSpec((cc.shape[0], d), lambda l, j: (0, 0)),
                  pl.BlockSpec((None, d, tn), lambda l, j: (l, 0, j)),
                  pl.BlockSpec((None, 1, tn), lambda l, j: (l, 0, j))],
        out_specs=pl.BlockSpec((None, cc.shape[0], tn), lambda l, j: (l, 0, j)),
        out_shape=jax.ShapeDtypeStruct((depth, cc.shape[0], n), F32),
        compiler_params=_cparams(2),
        name="adaln_mod",
    )(cc, w_mod, b_mod.reshape(depth, 1, n))


def _inproj_kernel(x_ref, g_ref, sc_ref, sh_ref, w_ref, o_ref, h_scr):
    @pl.when(pl.program_id(1) == 0)
    def _():
        xn = _rms(x_ref[...], x_ref.shape[-1]) * g_ref[...]
        h_scr[...] = _modulate(xn, sc_ref, sh_ref).astype(BF16)

    o_ref[...] = jnp.dot(h_scr[...], w_ref[...], preferred_element_type=F32).astype(o_ref.dtype)


def _inproj_call(xa, norm_g, sc, sh, w_in_p, n_lat_rows):
    rows, d = xa.shape
    n = w_in_p.shape[1]
    tm, tn = 1024, 1024
    nlat = n_lat_rows // tm
    seg = lambda i, j: (jnp.where(i >= nlat, 1, 0), 0, 0)
    return pl.pallas_call(
        _inproj_kernel,
        grid=(rows // tm, n // tn),
        in_specs=[pl.BlockSpec((tm, d), lambda i, j: (i, 0)),
                  pl.BlockSpec((1, d), lambda i, j: (0, 0)),
                  pl.BlockSpec((None, SUBLANES, d), seg),
                  pl.BlockSpec((None, SUBLANES, d), seg),
                  pl.BlockSpec((d, tn), lambda i, j: (0, j))],
        out_specs=pl.BlockSpec((tm, tn), lambda i, j: (i, j)),
        out_shape=jax.ShapeDtypeStruct((rows, n), BF16),
        scratch_shapes=[pltpu.VMEM((tm, d), BF16)],
        compiler_params=_cparams(2),
        name="norm_inproj",
    )(xa, norm_g.reshape(1, d), sc, sh, w_in_p)


def _prep_kernel(ymla_ref, ydq_ref, ydk_ref, cm_ref, sm_ref, cd_ref, sd_ref,
                 qn_g_ref, kvn_g_ref, q_g_ref, k_g_ref, dq_g_ref, dk_g_ref,
                 wuq_ref, wk_ref, wv_ref, pm_ref, pd_ref, ones_ref,
                 qm_ref, km_ref, vm_ref, qd_ref, kd_ref):
    q_rank = qn_g_ref.shape[-1]
    kv_rank = kvn_g_ref.shape[-1]
    ymla = ymla_ref[...].astype(F32)
    cq = ymla[:, :q_rank]
    ckv = ymla[:, q_rank:q_rank + kv_rank]
    kr = ymla[:, q_rank + kv_rank:q_rank + kv_rank + LANES]
    cm, sm = cm_ref[...], sm_ref[...]
    cd, sd = cd_ref[...], sd_ref[...]
    pm, pd = pm_ref[...], pd_ref[...]
    head_dim = MLA_NOPE + MLA_ROPE

    def rope(u, cos, sin, perm):
        return u * cos + jnp.dot(u.astype(BF16), perm, preferred_element_type=F32) * sin

    cqn = (_rms(cq, q_rank) * qn_g_ref[...]).astype(BF16)
    q = jnp.dot(cqn, wuq_ref[...], preferred_element_type=F32)
    q_gain = q_g_ref[...] * MLA_SCALE
    for h in range(MLA_HEADS):
        u = _rms(q[:, h * LANES:(h + 1) * LANES], head_dim) * q_gain
        qm_ref[:, h * LANES:(h + 1) * LANES] = rope(u, cm, sm, pm).astype(BF16)

    ckvn = (_rms(ckv, kv_rank) * kvn_g_ref[...]).astype(BF16)
    vm_ref[...] = jnp.dot(ckvn, wv_ref[...], preferred_element_type=F32).astype(BF16)
    kn = jnp.dot(ckvn, wk_ref[...], preferred_element_type=F32)
    k_gain = k_g_ref[...]
    for h in range(MLA_HEADS):
        u = _rms(kn[:, h * LANES:(h + 1) * LANES] + kr, head_dim) * k_gain
        km_ref[:, h * LANES:(h + 1) * LANES] = rope(u, cm, sm, pm).astype(BF16)

    ones_blk = ones_ref[...]

    def diff_norm_rope(y_ref, gain, o_ref):
        for h in range(DIFF_HEADS):
            x = y_ref[:, h * LANES:(h + 1) * LANES].astype(F32)
            ms = jnp.dot((x * x).astype(BF16), ones_blk, preferred_element_type=F32) * (1.0 / DIFF_HD)
            u = x * lax.rsqrt(ms + EPS) * gain
            o_ref[:, h * LANES:(h + 1) * LANES] = rope(u, cd, sd, pd).astype(BF16)

    diff_norm_rope(ydq_ref, dq_g_ref[...] * DIFF_SCALE, qd_ref)
    diff_norm_rope(ydk_ref, dk_g_ref[...], kd_ref)


def _prep_call(y, tabs, gains, weights, consts):
    rows = y.shape[0]
    d = 1024
    tm = 256
    row_blk = lambda w: pl.BlockSpec((tm, w), lambda i: (i, 0))
    col_blk = lambda c: pl.BlockSpec((tm, d), lambda i, c=c: (i, c))
    full = lambda a: pl.BlockSpec(a.shape, lambda i: (0,) * a.ndim)
    wq = MLA_HEADS * LANES
    out_shapes = [jax.ShapeDtypeStruct((rows, wq), BF16), jax.ShapeDtypeStruct((rows, wq), BF16),
                  jax.ShapeDtypeStruct((rows, MLA_HEADS * MLA_V), BF16),
                  jax.ShapeDtypeStruct((rows, d), BF16), jax.ShapeDtypeStruct((rows, d), BF16)]
    return pl.pallas_call(
        _prep_kernel,
        grid=(rows // tm,),
        in_specs=[col_blk(COL_MLA), col_blk(COL_DQ), col_blk(COL_DK)]
                 + [row_blk(LANES)] * 4
                 + [full(a) for a in gains] + [full(a) for a in weights] + [full(a) for a in consts],
        out_specs=[row_blk(s.shape[1]) for s in out_shapes],
        out_shape=out_shapes,
        compiler_params=_cparams(1),
        name="attn_prep",
    )(y, y, y, *tabs, *gains, *weights, *consts)


def _lru_kernel(pf_ref, cf_ref, nf_ref, pb_ref, cb_ref, nb_ref,
                cw_ref, cbias_ref, wa_ref, wi_ref, ba_ref, bi_ref, lam_ref,
                yf_ref, yb_ref, h_scr, a_scr, b_scr, y_scr, *, nkc, nlat):
    s = pl.program_id(0)
    tm, w = cf_ref.shape
    steps = tm // SUBLANES
    blk_w = w // RNN_BLOCKS

    @pl.when(s == 0)
    def _():
        h_scr[...] = jnp.zeros_like(h_scr)

    in_ctx = s < nkc
    ci = jnp.where(in_ctx, s, s - nkc)
    nseq = jnp.where(in_ctx, nkc, nlat)
    at_start = ci == 0
    at_end = ci == nseq - 1

    def direction(r, prev_ref, cur_ref, next_ref, has_prev, has_next, y_ref):
        prev = jnp.where(has_prev, prev_ref[...].astype(F32), 0.0)
        nxt = jnp.where(has_next, next_ref[...].astype(F32), 0.0)
        win = jnp.concatenate([prev, cur_ref[...].astype(F32), nxt], axis=0)
        base = BF16_ROWS
        xc = cbias_ref[...]
        for j in range(CONV_W):
            off = base + (j - CONV_W // 2) * SUBLANES
            xc = xc + win[off:off + tm] * cw_ref[j:j + 1, :]
        gr, gi = [], []
        for n in range(RNN_BLOCKS):
            xb = xc[:, n * blk_w:(n + 1) * blk_w].astype(BF16)
            gr.append(jnp.dot(xb, wa_ref[r * RNN_BLOCKS + n], preferred_element_type=F32))
            gi.append(jnp.dot(xb, wi_ref[r * RNN_BLOCKS + n], preferred_element_type=F32))
        rg = _sigmoid(jnp.concatenate(gr, axis=1) + ba_ref[r])
        ig = _sigmoid(jnp.concatenate(gi, axis=1) + bi_ref[r])
        lam = lam_ref[r]
        softplus_neg = jnp.maximum(-lam, 0.0) + jnp.log(1.0 + jnp.exp(-jnp.abs(lam)))
        a = jnp.exp(-LRU_C * rg * softplus_neg)
        a_scr[...] = a
        b_scr[...] = jnp.sqrt(1.0 - a * a) * ig * xc
        h = h_scr[r]
        order = range(steps) if r == 0 else range(steps - 1, -1, -1)
        for j in order:
            rows = pl.ds(j * SUBLANES, SUBLANES)
            h = a_scr[rows, :] * h + b_scr[rows, :]
            y_scr[rows, :] = h
        h_scr[r] = h
        y_ref[...] = y_scr[...].astype(y_ref.dtype)

    direction(0, pf_ref, cf_ref, nf_ref, jnp.logical_not(at_start), jnp.logical_not(at_end), yf_ref)
    direction(1, pb_ref, cb_ref, nb_ref, jnp.logical_not(at_end), jnp.logical_not(at_start), yb_ref)


def _lru_call(y, conv_w, conv_b, wa, wi, ba, bi, lam, n_lat_rows):
    rows = y.shape[0]
    w = 1024
    tm = 512
    per = tm // BF16_ROWS
    nlat = n_lat_rows // tm
    nkc = rows // tm - nlat
    last16 = rows // BF16_ROWS - 1

    def chunk_f(s):
        return jnp.where(s < nkc, nlat + s, s - nkc)

    def chunk_b(s):
        return jnp.where(s < nkc, nlat + nkc - 1 - s, nlat - 1 - (s - nkc))

    def specs(chunk):
        return [pl.BlockSpec((BF16_ROWS, w), lambda s: (jnp.maximum(chunk(s) * per - 1, 0), COL_RX)),
                pl.BlockSpec((tm, w), lambda s: (chunk(s), COL_RX)),
                pl.BlockSpec((BF16_ROWS, w), lambda s: (jnp.minimum((chunk(s) + 1) * per, last16), COL_RX))]

    full = lambda a: pl.BlockSpec(a.shape, lambda s: (0,) * a.ndim)
    params = [conv_w, conv_b.reshape(1, w), wa, wi, ba.reshape(2, 1, w), bi.reshape(2, 1, w),
              lam.reshape(2, 1, w)]
    return pl.pallas_call(
        functools.partial(_lru_kernel, nkc=nkc, nlat=nlat),
        grid=(rows // tm,),
        in_specs=specs(chunk_f) + specs(chunk_b) + [full(a) for a in params],
        out_specs=[pl.BlockSpec((tm, w), lambda s: (chunk_f(s), 0)),
                   pl.BlockSpec((tm, w), lambda s: (chunk_b(s), 0))],
        out_shape=[jax.ShapeDtypeStruct((rows, w), BF16)] * 2,
        scratch_shapes=[pltpu.VMEM((2, SUBLANES, w), F32), pltpu.VMEM((tm, w), F32),
                        pltpu.VMEM((tm, w), F32), pltpu.VMEM((tm, w), F32)],
        compiler_params=_cparams(1),
        name="conv_rglru_scan",
    )(y, y, y, y, y, y, *params)


def _softmax_pv(q, k, v):
    s = lax.dot_general(q, k, (((1,), (1,)), ((), ())), preferred_element_type=F32)
    e = jnp.exp(s - jnp.max(s, axis=-1, keepdims=True))
    l = jnp.sum(e, axis=-1, keepdims=True)
    return jnp.dot(e.astype(BF16), v, preferred_element_type=F32), l


def _mla_attn_kernel(q_ref, k_ref, v_ref, o_ref, *, n_lat_q, t_lat):
    qi = pl.program_id(2)
    lane = lax.broadcasted_iota(jnp.int32, (q_ref.shape[0], LANES), 1)

    def run(lo):
        for p in range(q_ref.shape[1] // (2 * LANES)):
            vp = v_ref[lo:, p * LANES:(p + 1) * LANES]
            outs = []
            for hh in (2 * p, 2 * p + 1):
                cols = slice(hh * LANES, (hh + 1) * LANES)
                o, l = _softmax_pv(q_ref[:, cols], k_ref[lo:, cols], vp)
                outs.append(o * (1.0 / l))
            o_ref[:, p * LANES:(p + 1) * LANES] = jnp.where(lane < MLA_V, outs[0], outs[1]).astype(o_ref.dtype)

    @pl.when(qi < n_lat_q)
    def _():
        run(0)

    @pl.when(qi >= n_lat_q)
    def _():
        run(t_lat)


def _diff_attn_kernel(q_ref, k_ref, v_ref, dl_ref, o_ref, *, n_lat_q, t_lat, lam_init):
    qi = pl.program_id(2)
    lane = lax.broadcasted_iota(jnp.int32, (q_ref.shape[0], LANES), 1)
    dl = dl_ref[...]
    lam = (jnp.exp(jnp.sum(dl[0:1] * dl[1:2], axis=-1, keepdims=True))
           - jnp.exp(jnp.sum(dl[2:3] * dl[3:4], axis=-1, keepdims=True)) + lam_init)

    def run(lo):
        for h in range(q_ref.shape[1] // LANES):
            cols = slice(h * LANES, (h + 1) * LANES)
            q = q_ref[:, cols].astype(F32)
            k = k_ref[lo:, cols]
            v = v_ref[lo:, cols]
            o0, l0 = _softmax_pv(jnp.where(lane < DIFF_HD, q, 0.0).astype(BF16), k, v)
            o1, l1 = _softmax_pv(jnp.where(lane >= DIFF_HD, q, 0.0).astype(BF16), k, v)
            o_ref[:, cols] = (o0 * (1.0 / l0) - lam * (o1 * (1.0 / l1))).astype(o_ref.dtype)

    @pl.when(qi < n_lat_q)
    def _():
        run(0)

    @pl.when(qi >= n_lat_q)
    def _():
        run(t_lat)


def _attn_call(kernel, q2, k2, v2, extra, *, batch, heads_per_step, qw, vw, ow, n_heads,
               t_lat, t_ctx, need_ctx, vcol0, name):
    s_all = t_lat + t_ctx
    tq = t_ctx
    n_lat_q = t_lat // tq
    nq = n_lat_q + (1 if need_ctx else 0)
    groups = n_heads // heads_per_step
    out_rows = s_all if need_ctx else t_lat
    full = lambda a: pl.BlockSpec(a.shape, lambda b, g, i: (0,) * a.ndim)
    return pl.pallas_call(
        functools.partial(kernel, n_lat_q=n_lat_q, t_lat=t_lat),
        grid=(batch, groups, nq),
        in_specs=[pl.BlockSpec((tq, heads_per_step * qw), lambda b, g, i: (i, b * groups + g)),
                  pl.BlockSpec((s_all, heads_per_step * qw), lambda b, g, i: (0, b * groups + g)),
                  pl.BlockSpec((s_all, heads_per_step * vw), lambda b, g, i: (0, vcol0(b) + g))]
                 + [full(a) for a in extra],
        out_specs=pl.BlockSpec((tq, heads_per_step * ow), lambda b, g, i: (i, b * groups + g)),
        out_shape=jax.ShapeDtypeStruct((out_rows, batch * n_heads * ow), BF16),
        compiler_params=_cparams(3),
        name=name,
    )(q2, k2, v2, *extra)


def _merge_kernel(x_ref, yf_ref, yb_ref, rg_ref, ob_ref, oc_ref, mg_ref, g1_ref, sub_g_ref,
                  wa_ref, wb_ref, wc_ref, wo_ref, o_ref, *, lam_init):
    d = x_ref.shape[-1]
    ya = yf_ref[...].astype(F32) + yb_ref[...].astype(F32)
    za = (ya * _gelu_tanh(rg_ref[...].astype(F32))).astype(BF16)
    br_a = jnp.dot(za, wa_ref[...], preferred_element_type=F32)
    br_b = jnp.dot(ob_ref[...], wb_ref[...], preferred_element_type=F32)
    oc = oc_ref[...].astype(F32)
    sub_gain = sub_g_ref[...] * (1.0 - lam_init)
    oc_n = jnp.concatenate(
        [_rms(oc[:, h * DIFF_V:(h + 1) * DIFF_V], DIFF_V) * sub_gain for h in range(DIFF_HEADS)], axis=1)
    br_c = jnp.dot(oc_n.astype(BF16), wc_ref[...], preferred_element_type=F32)
    mg = mg_ref[...].astype(F32)
    mix = (_sigmoid(mg[:, :d]) * br_a + _sigmoid(mg[:, d:2 * d]) * br_b
           + _sigmoid(mg[:, 2 * d:]) * br_c)
    m = jnp.dot(mix.astype(BF16), wo_ref[...], preferred_element_type=F32)
    o_ref[...] = x_ref[...] + _gate_rows(m, g1_ref)


def _merge_call(xa, yf, yb, y, ob, oc, g1, sub_g, wa, wb, wc, wo, n_lat_rows, n_rows, lam_init):
    d = xa.shape[1]
    tm = 256
    nlat = n_lat_rows // tm
    blk = lambda: pl.BlockSpec((tm, d), lambda i: (i, 0))
    full = lambda a: pl.BlockSpec(a.shape, lambda i: (0,) * a.ndim)
    return pl.pallas_call(
        functools.partial(_merge_kernel, lam_init=lam_init),
        grid=(n_rows // tm,),
        in_specs=[blk(), blk(), blk(),
                  pl.BlockSpec((tm, d), lambda i: (i, COL_RG)),
                  blk(), blk(),
                  pl.BlockSpec((tm, N_BRANCH * d), lambda i: (i, COL_MG // N_BRANCH)),
                  pl.BlockSpec((None, SUBLANES, d), lambda i: (jnp.where(i >= nlat, 1, 0), 0, 0)),
                  full(sub_g), full(wa), full(wb), full(wc), full(wo)],
        out_specs=blk(),
        out_shape=jax.ShapeDtypeStruct((n_rows, d), F32),
        compiler_params=_cparams(1),
        name="branch_merge",
    )(xa, yf, yb, y, ob, oc, y, g1, sub_g, wa, wb, wc, wo)


def _ffn_kernel(x_ref, g_ref, sc_ref, sh_ref, g2_ref, win_ref, wout_ref, o_ref, h_scr, acc_scr):
    j = pl.program_id(1)

    @pl.when(j == 0)
    def _():
        xn = _rms(x_ref[...], x_ref.shape[-1]) * g_ref[...]
        h_scr[...] = _modulate(xn, sc_ref, sh_ref).astype(BF16)
        acc_scr[...] = jnp.zeros_like(acc_scr)

    gu = jnp.dot(h_scr[...], win_ref[...], preferred_element_type=F32)
    th = gu.shape[1] // 2
    act = (_silu(gu[:, :th]) * gu[:, th:]).astype(BF16)
    acc_scr[...] += jnp.dot(act, wout_ref[...], preferred_element_type=F32)

    @pl.when(j == pl.num_programs(1) - 1)
    def _():
        o_ref[...] = x_ref[...] + _gate_rows(acc_scr[...], g2_ref)


def _ffn_call(xa, norm_g, sc, sh, g2, w_in_p, w_out, n_lat_rows, th):
    rows, d = xa.shape
    hidden = w_out.shape[0]
    tm = 512
    nlat = n_lat_rows // tm
    seg = lambda i, j: (jnp.where(i >= nlat, 1, 0), 0, 0)
    return pl.pallas_call(
        _ffn_kernel,
        grid=(rows // tm, hidden // th),
        in_specs=[pl.BlockSpec((tm, d), lambda i, j: (i, 0)),
                  pl.BlockSpec((1, d), lambda i, j: (0, 0)),
                  pl.BlockSpec((None, SUBLANES, d), seg),
                  pl.BlockSpec((None, SUBLANES, d), seg),
                  pl.BlockSpec((None, SUBLANES, d), seg),
                  pl.BlockSpec((d, 2 * th), lambda i, j: (0, j)),
                  pl.BlockSpec((th, d), lambda i, j: (j, 0))],
        out_specs=pl.BlockSpec((tm, d), lambda i, j: (i, 0)),
        out_shape=jax.ShapeDtypeStruct((rows, d), F32),
        scratch_shapes=[pltpu.VMEM((tm, d), BF16), pltpu.VMEM((tm, d), F32)],
        compiler_params=_cparams(2),
        name="norm_swiglu_ffn",
    )(xa, norm_g.reshape(1, d), sc, sh, g2, w_in_p, w_out)


def _rope_lane_tables(t_lat, t_ctx, batch, rot_dim, lane_of_pair, copies):
    rows_n = t_lat // GRID_W
    row_ids = jnp.repeat(jnp.arange(rows_n, dtype=F32), GRID_W)
    col_ids = jnp.tile(jnp.arange(GRID_W, dtype=F32), rows_n)
    n = rot_dim // 4
    freqs = ROPE_BASE ** (-jnp.arange(n, dtype=F32) / n)
    ang = jnp.concatenate([row_ids[:, None] * freqs, col_ids[:, None] * freqs], axis=-1)
    cos, sin = jnp.cos(ang), jnp.sin(ang)
    npairs = rot_dim // 2
    sel_c = np.zeros((npairs, LANES), np.float32)
    sel_s = np.zeros((npairs, LANES), np.float32)
    base_c = np.ones((LANES,), np.float32)
    for off in copies:
        for i in range(npairs):
            lane = off + lane_of_pair(i)
            sel_c[i, lane] = sel_c[i, lane + 1] = 1.0
            sel_s[i, lane] = -1.0
            sel_s[i, lane + 1] = 1.0
            base_c[lane] = base_c[lane + 1] = 0.0
    c_lat = cos @ sel_c + base_c[None, :]
    s_lat = sin @ sel_s
    c_all = jnp.concatenate([c_lat, jnp.ones((t_ctx, LANES), F32)], axis=0)
    s_all = jnp.concatenate([s_lat, jnp.zeros((t_ctx, LANES), F32)], axis=0)
    expand = lambda a: jnp.repeat(a, batch, axis=0)
    return expand(c_all), expand(s_all)


def _pair_swap_matrix(rot_dim, lane_of_pair, copies):
    p = np.zeros((LANES, LANES), np.float32)
    for off in copies:
        for i in range(rot_dim // 2):
            lane = off + lane_of_pair(i)
            p[lane + 1, lane] = 1.0
            p[lane, lane + 1] = 1.0
    return jnp.asarray(p, BF16)


def _segmented(mod_l, idx, batch):
    d = mod_l.shape[1] // 6
    part = mod_l[:, idx * d:(idx + 1) * d]
    return jnp.stack([part[:batch], jnp.broadcast_to(part[batch:batch + 1], (batch, d))])


def kernel(x, c, ctx, c_ctx, w_mod, b_mod, norm1_g, norm2_g, w_in, conv_w, conv_b, lru_wa, lru_ba, lru_wi, lru_bi, lru_lambda, mla_qn_g, mla_w_uq, mla_kvn_g, mla_w_ukv, mla_q_g, mla_k_g, diff_q_g, diff_k_g, diff_lambda, diff_subln_g, w_br_a, w_br_b, w_br_c, w_out, w_ffn_in, w_ffn_out):
    batch, t_lat, d = x.shape
    t_ctx = ctx.shape[1]
    depth = w_mod.shape[0]
    assert batch == SUBLANES and d == 1024
    n_lat_rows = t_lat * batch
    q_rank = mla_qn_g.shape[1]
    kv_rank = mla_kvn_g.shape[1]
    hidden = w_ffn_out.shape[1]
    th = hidden // 2

    xa = jnp.concatenate([jnp.transpose(x, (1, 0, 2)).reshape(n_lat_rows, d),
                          jnp.transpose(ctx, (1, 0, 2)).reshape(t_ctx * batch, d)], axis=0)

    cc = jnp.concatenate([c, c_ctx[None, :], jnp.zeros((BF16_ROWS - batch - 1, d), F32)], axis=0)
    mod = _mod_call(cc, w_mod, b_mod)

    mla_pair_lane = lambda i: MLA_NOPE + 2 * i
    diff_pair_lane = lambda i: 2 * i
    cm, sm = _rope_lane_tables(t_lat, t_ctx, batch, MLA_ROPE, mla_pair_lane, (0,))
    cd, sd = _rope_lane_tables(t_lat, t_ctx, batch, DIFF_HD, diff_pair_lane, (0, DIFF_HD))
    pm = _pair_swap_matrix(MLA_ROPE, mla_pair_lane, (0,))
    pd = _pair_swap_matrix(DIFF_HD, diff_pair_lane, (0, DIFF_HD))
    ones_np = np.zeros((LANES, LANES), np.float32)
    ones_np[:DIFF_HD, :DIFF_HD] = 1.0
    ones_np[DIFF_HD:, DIFF_HD:] = 1.0
    ones_blk = jnp.asarray(ones_np, BF16)

    offs = np.cumsum([0, d, d, q_rank, kv_rank, MLA_ROPE, d, d, d, N_BRANCH * d])
    o_rx, o_rg, o_cq, o_ckv, o_kr, o_dq, o_dk, o_dv, o_mg = offs[:9]
    head_dim = MLA_NOPE + MLA_ROPE

    for l in range(depth):
        need_ctx = l < depth - 1
        lam_init = 0.8 - 0.6 * math.exp(-0.3 * l)
        wl = w_in[l]
        mla_cols = jnp.concatenate(
            [wl[:, o_cq:o_cq + q_rank], wl[:, o_ckv:o_ckv + kv_rank],
             jnp.zeros((d, MLA_NOPE), F32), wl[:, o_kr:o_kr + MLA_ROPE],
             jnp.zeros((d, d - q_rank - kv_rank - head_dim), F32)], axis=1)
        w_in_p = jnp.concatenate(
            [mla_cols, wl[:, o_rx:o_rx + d], wl[:, o_rg:o_rg + d], wl[:, o_dq:o_dq + d],
             wl[:, o_dk:o_dk + d], wl[:, o_dv:o_dv + d], wl[:, o_mg:o_mg + N_BRANCH * d]],
            axis=1).astype(BF16)
        w_uq_p = jnp.pad(mla_w_uq[l].reshape(q_rank, MLA_HEADS, head_dim),
                         ((0, 0), (0, 0), (0, LANES - head_dim))).reshape(q_rank, MLA_HEADS * LANES).astype(BF16)
        w_ukv = mla_w_ukv[l].reshape(kv_rank, MLA_HEADS, MLA_NOPE + MLA_V)
        w_k_p = jnp.pad(w_ukv[:, :, :MLA_NOPE],
                        ((0, 0), (0, 0), (0, LANES - MLA_NOPE))).reshape(kv_rank, MLA_HEADS * LANES).astype(BF16)
        w_v_p = w_ukv[:, :, MLA_NOPE:].reshape(kv_rank, MLA_HEADS * MLA_V).astype(BF16)
        pad_gain = lambda g: jnp.pad(g, (0, LANES - head_dim)).reshape(1, LANES)
        gains = [mla_qn_g[l].reshape(1, q_rank), mla_kvn_g[l].reshape(1, kv_rank),
                 pad_gain(mla_q_g[l]), pad_gain(mla_k_g[l]),
                 jnp.tile(diff_q_g[l], 2).reshape(1, LANES), jnp.tile(diff_k_g[l], 2).reshape(1, LANES)]
        ffn_in = w_ffn_in[l]
        w_ffn_in_p = jnp.concatenate(
            [ffn_in[:, j * th:(j + 1) * th] if half == 0 else ffn_in[:, hidden + j * th:hidden + (j + 1) * th]
             for j in range(hidden // th) for half in (0, 1)], axis=1).astype(BF16)

        sh1, sc1, g1, sh2, sc2, g2 = [_segmented(mod[l], i, batch) for i in range(6)]

        y = _inproj_call(xa, norm1_g[l], sc1, sh1, w_in_p, n_lat_rows)
        yf, yb = _lru_call(y, conv_w[l], conv_b[l],
                           lru_wa[l].reshape(2 * RNN_BLOCKS, d // RNN_BLOCKS, d // RNN_BLOCKS).astype(BF16),
                           lru_wi[l].reshape(2 * RNN_BLOCKS, d // RNN_BLOCKS, d // RNN_BLOCKS).astype(BF16),
                           lru_ba[l], lru_bi[l], lru_lambda[l], n_lat_rows)
        qm, km, vm, qd, kd = _prep_call(y, (cm, sm, cd, sd), gains, (w_uq_p, w_k_p, w_v_p),
                                        (pm, pd, ones_blk))

        s_all = t_lat + t_ctx
        view = lambda a: a.reshape(s_all, batch * a.shape[1])
        o_b = _attn_call(_mla_attn_kernel, view(qm), view(km), view(vm), (),
                         batch=batch, heads_per_step=4, qw=LANES, vw=MLA_V, ow=MLA_V, n_heads=MLA_HEADS,
                         t_lat=t_lat, t_ctx=t_ctx, need_ctx=need_ctx,
                         vcol0=lambda b: b * (MLA_HEADS // 4), name="mla_attention")
        n_ycol = y.shape[1] // d
        o_c = _attn_call(functools.partial(_diff_attn_kernel, lam_init=lam_init),
                         view(qd), view(kd), view(y), (diff_lambda[l],),
                         batch=batch, heads_per_step=2, qw=LANES, vw=DIFF_V, ow=DIFF_V, n_heads=DIFF_HEADS,
                         t_lat=t_lat, t_ctx=t_ctx, need_ctx=need_ctx,
                         vcol0=lambda b: (b * n_ycol + COL_DV) * (DIFF_HEADS // 2), name="diff_attention")

        n_rows = xa.shape[0] if need_ctx else n_lat_rows
        out_rows = lambda a: a.reshape(-1, d)
        xa = _merge_call(xa, yf, yb, y, out_rows(o_b), out_rows(o_c), g1,
                         diff_subln_g[l].reshape(1, DIFF_V), w_br_a[l].astype(BF16), w_br_b[l].astype(BF16),
                         w_br_c[l].astype(BF16), w_out[l].astype(BF16), n_lat_rows, n_rows, lam_init)
        xa = _ffn_call(xa, norm2_g[l], sc2, sh2, g2, w_ffn_in_p, w_ffn_out[l].astype(BF16), n_lat_rows, th)

    return jnp.transpose(xa[:n_lat_rows].reshape(t_lat, batch, d), (1, 0, 2))
```

```python
import functools
import math

import numpy as np
import jax
import jax.numpy as jnp
from jax import lax
from jax.experimental import pallas as pl
from jax.experimental.pallas import tpu as pltpu

F32 = jnp.float32
BF16 = jnp.bfloat16

EPS = 1e-6
GRID_W = 64
ROPE_BASE = 10000.0
LRU_C = 8.0
CONV_W = 4
RNN_BLOCKS = 8
MLA_HEADS = 16
MLA_NOPE = 64
MLA_ROPE = 32
MLA_V = 64
MLA_SCALE = (MLA_NOPE + MLA_ROPE) ** -0.5
DIFF_HEADS = 8
DIFF_HD = 64
DIFF_V = 2 * DIFF_HD
DIFF_SCALE = DIFF_HD ** -0.5
N_BRANCH = 3
LOG2E = math.log2(math.e)

LANES = 128
SUBLANES = 8
BF16_ROWS = 16
VMEM_LIMIT = 56 * 1024 * 1024

COL_MLA, COL_RX, COL_RG, COL_DQ, COL_DK, COL_DV, COL_MG = 0, 1, 2, 3, 4, 5, 6


def _cparams(n_axes):
    return pltpu.CompilerParams(dimension_semantics=("arbitrary",) * n_axes,
                                vmem_limit_bytes=VMEM_LIMIT)


def _full(a, n_axes):
    return pl.BlockSpec(a.shape, lambda *_: (0,) * a.ndim)


def _sigmoid(x):
    return 1.0 / (1.0 + jnp.exp(-x))


def _silu(x):
    return x * _sigmoid(x)


def _gelu_tanh(x):
    return 0.5 * x * (1.0 + jnp.tanh(math.sqrt(2.0 / math.pi) * (x + 0.044715 * (x * x * x))))


def _rms(x, denom):
    ms = jnp.sum(x * x, axis=-1, keepdims=True) * (1.0 / denom)
    return x * lax.rsqrt(ms + EPS)


def _mod_kernel(c_ref, w_ref, b_ref, o_ref):
    s = _silu(c_ref[...]).astype(BF16)
    o_ref[...] = jnp.dot(s, w_ref[...].astype(BF16), preferred_element_type=F32) + b_ref[...]


def _mod_call(cc, w_mod, b_mod):
    depth, d, n = w_mod.shape
    tn = 1024
    return pl.pallas_call(
        _mod_kernel,
        grid=(depth, n // tn),
        in_specs=[pl.BlockSpec((cc.shape[0], d), lambda l, j: (0, 0)),
                  pl.BlockSpec((None, d, tn), lambda l, j: (l, 0, j)),
                  pl.BlockSpec((None, 1, tn), lambda l, j: (l, 0, j))],
        out_specs=pl.BlockSpec((None, cc.shape[0], tn), lambda l, j: (l, 0, j)),
        out_shape=jax.ShapeDtypeStruct((depth, cc.shape[0], n), F32),
        compiler_params=_cparams(2),
        name="adaln_mod",
    )(cc, w_mod, b_mod.reshape(depth, 1, n))


def _inproj_kernel(x_ref, g_ref, sc_ref, sh_ref, w_ref, o_ref, h_scr):
    @pl.when(pl.program_id(1) == 0)
    def _():
        xn = _rms(x_ref[...], x_ref.shape[-1]) * g_ref[...]
        h_scr[...] = (xn * (1.0 + sc_ref[...]) + sh_ref[...]).astype(BF16)

    o_ref[...] = jnp.dot(h_scr[...], w_ref[...], preferred_element_type=F32).astype(o_ref.dtype)


def _inproj_call(xs, norm_g, sc, sh, w_in_p):
    rows, d = xs.shape
    n = w_in_p.shape[1]
    tm, tn = min(1024, rows // sc.shape[0]), 1024
    per_seq = rows // tm // sc.shape[0]
    seq = lambda i, j: (i // per_seq, 0, 0)
    return pl.pallas_call(
        _inproj_kernel,
        grid=(rows // tm, n // tn),
        in_specs=[pl.BlockSpec((tm, d), lambda i, j: (i, 0)),
                  pl.BlockSpec((1, d), lambda i, j: (0, 0)),
                  pl.BlockSpec((None, 1, d), seq),
                  pl.BlockSpec((None, 1, d), seq),
                  pl.BlockSpec((d, tn), lambda i, j: (0, j))],
        out_specs=pl.BlockSpec((tm, tn), lambda i, j: (i, j)),
        out_shape=jax.ShapeDtypeStruct((rows, n), BF16),
        scratch_shapes=[pltpu.VMEM((tm, d), BF16)],
        compiler_params=_cparams(2),
        name="norm_inproj",
    )(xs, norm_g.reshape(1, d), sc, sh, w_in_p)


def _prep_kernel(*refs, use_rope):
    if use_rope:
        (ymla_ref, ydq_ref, ydk_ref, cm_ref, sm_ref, cd_ref, sd_ref,
         qn_g_ref, kvn_g_ref, q_g_ref, k_g_ref, dq_g_ref, dk_g_ref,
         wuq_ref, wk_ref, wv_ref, ones_ref, pm_ref, pd_ref,
         qm_ref, km_ref, vm_ref, qd_ref, kd_ref) = refs
        cm, sm, cd, sd = cm_ref[...], sm_ref[...], cd_ref[...], sd_ref[...]
        pm, pd = pm_ref[...], pd_ref[...]
    else:
        (ymla_ref, ydq_ref, ydk_ref,
         qn_g_ref, kvn_g_ref, q_g_ref, k_g_ref, dq_g_ref, dk_g_ref,
         wuq_ref, wk_ref, wv_ref, ones_ref,
         qm_ref, km_ref, vm_ref, qd_ref, kd_ref) = refs
        cm = sm = cd = sd = pm = pd = None
    q_rank = qn_g_ref.shape[-1]
    kv_rank = kvn_g_ref.shape[-1]
    ymla = ymla_ref[...].astype(F32)
    cq = ymla[:, :q_rank]
    ckv = ymla[:, q_rank:q_rank + kv_rank]
    kr = ymla[:, q_rank + kv_rank:q_rank + kv_rank + LANES]
    head_dim = MLA_NOPE + MLA_ROPE

    def rope(u, cos, sin, perm):
        if not use_rope:
            return u
        return u * cos + jnp.dot(u.astype(BF16), perm, preferred_element_type=F32) * sin

    cqn = (_rms(cq, q_rank) * qn_g_ref[...]).astype(BF16)
    q = jnp.dot(cqn, wuq_ref[...], preferred_element_type=F32)
    q_gain = q_g_ref[...] * (MLA_SCALE * LOG2E)
    for h in range(MLA_HEADS):
        u = _rms(q[:, h * LANES:(h + 1) * LANES], head_dim) * q_gain
        qm_ref[:, h * LANES:(h + 1) * LANES] = rope(u, cm, sm, pm).astype(BF16)

    ckvn = (_rms(ckv, kv_rank) * kvn_g_ref[...]).astype(BF16)
    vm_ref[...] = jnp.dot(ckvn, wv_ref[...], preferred_element_type=F32).astype(BF16)
    kn = jnp.dot(ckvn, wk_ref[...], preferred_element_type=F32)
    k_gain = k_g_ref[...]
    for h in range(MLA_HEADS):
        u = _rms(kn[:, h * LANES:(h + 1) * LANES] + kr, head_dim) * k_gain
        km_ref[:, h * LANES:(h + 1) * LANES] = rope(u, cm, sm, pm).astype(BF16)

    ones_blk = ones_ref[...]

    def diff_norm_rope(y_ref, gain, o_ref):
        for h in range(DIFF_HEADS):
            x = y_ref[:, h * LANES:(h + 1) * LANES].astype(F32)
            ms = jnp.dot((x * x).astype(BF16), ones_blk, preferred_element_type=F32) * (1.0 / DIFF_HD)
            u = x * lax.rsqrt(ms + EPS) * gain
            o_ref[:, h * LANES:(h + 1) * LANES] = rope(u, cd, sd, pd).astype(BF16)

    diff_norm_rope(ydq_ref, dq_g_ref[...] * (DIFF_SCALE * LOG2E), qd_ref)
    diff_norm_rope(ydk_ref, dk_g_ref[...], kd_ref)


def _prep_call(y, tabs, gains, weights, consts, t_seq):
    rows = y.shape[0]
    d = 1024
    tm = 256
    use_rope = len(tabs) > 0
    per_seq = t_seq // tm
    row_blk = lambda w: pl.BlockSpec((tm, w), lambda i: (i, 0))
    col_blk = lambda c: pl.BlockSpec((tm, d), lambda i, c=c: (i, c))
    tab_blk = pl.BlockSpec((tm, LANES), lambda i: (i % per_seq, 0))
    wq = MLA_HEADS * LANES
    out_shapes = [jax.ShapeDtypeStruct((rows, wq), BF16), jax.ShapeDtypeStruct((rows, wq), BF16),
                  jax.ShapeDtypeStruct((rows, MLA_HEADS * MLA_V), BF16),
                  jax.ShapeDtypeStruct((rows, d), BF16), jax.ShapeDtypeStruct((rows, d), BF16)]
    small = list(gains) + list(weights) + list(consts)
    return pl.pallas_call(
        functools.partial(_prep_kernel, use_rope=use_rope),
        grid=(rows // tm,),
        in_specs=[col_blk(COL_MLA), col_blk(COL_DQ), col_blk(COL_DK)]
                 + [tab_blk] * len(tabs) + [_full(a, 1) for a in small],
        out_specs=[row_blk(s.shape[1]) for s in out_shapes],
        out_shape=out_shapes,
        compiler_params=_cparams(1),
        name="attn_prep_rope" if use_rope else "attn_prep",
    )(y, y, y, *tabs, *small)


def _lru_kernel(pf_ref, cf_ref, nf_ref, pb_ref, cb_ref, nb_ref, h0_ref,
                cw_ref, cbias_ref, wa_ref, wi_ref, ba_ref, bi_ref, lam_ref,
                yf_ref, yb_ref, hfin_ref, h_scr, win_scr, a_scr, b_scr, y_scr):
    k = pl.program_id(0)
    nk = pl.num_programs(0)
    batch, tt, w = cf_ref.shape
    halo = pf_ref.shape[1]
    tm = tt * batch
    blk_w = w // RNN_BLOCKS

    @pl.when(k == 0)
    def _():
        h_scr[...] = h0_ref[...]

    def interleave(src_ref, row0, n_steps, keep):
        for b in range(batch):
            v = src_ref[b].astype(F32)
            if keep is not None:
                v = jnp.where(keep, v, 0.0)
            for n in range(RNN_BLOCKS):
                win_scr[n, pl.ds(row0 + b, n_steps, stride=batch), :] = v[:, n * blk_w:(n + 1) * blk_w]

    def direction(r, prev_ref, cur_ref, next_ref, has_prev, has_next, y_ref):
        interleave(prev_ref, 0, halo, has_prev)
        interleave(cur_ref, halo * batch, tt, None)
        interleave(next_ref, (halo + tt) * batch, halo, has_next)
        base = halo * batch
        for n in range(RNN_BLOCKS):
            lanes = slice(n * blk_w, (n + 1) * blk_w)
            xc = cbias_ref[:, lanes]
            for j in range(CONV_W):
                off = base + (j - CONV_W // 2) * batch
                xc = xc + win_scr[n, off:off + tm, :] * cw_ref[j:j + 1, lanes]
            xb = xc.astype(BF16)
            gr = jnp.dot(xb, wa_ref[r * RNN_BLOCKS + n], preferred_element_type=F32) + ba_ref[r][:, lanes]
            gi = jnp.dot(xb, wi_ref[r * RNN_BLOCKS + n], preferred_element_type=F32) + bi_ref[r][:, lanes]
            lam = lam_ref[r][:, lanes]
            softplus_neg = jnp.maximum(-lam, 0.0) + jnp.log(1.0 + jnp.exp(-jnp.abs(lam)))
            a = jnp.exp(-LRU_C * _sigmoid(gr) * softplus_neg)
            a_scr[n] = a
            b_scr[n] = jnp.sqrt(1.0 - a * a) * _sigmoid(gi) * xc
        h = [h_scr[r, :, n * blk_w:(n + 1) * blk_w] for n in range(RNN_BLOCKS)]
        order = range(tt) if r == 0 else range(tt - 1, -1, -1)
        for j in order:
            rows = pl.ds(j * batch, batch)
            for n in range(RNN_BLOCKS):
                h[n] = a_scr[n, rows, :] * h[n] + b_scr[n, rows, :]
                y_scr[n, rows, :] = h[n]
        for n in range(RNN_BLOCKS):
            h_scr[r, :, n * blk_w:(n + 1) * blk_w] = h[n]
        for b in range(batch):
            for n in range(RNN_BLOCKS):
                y_ref[b, :, n * blk_w:(n + 1) * blk_w] = (
                    y_scr[n, pl.ds(b, tt, stride=batch), :].astype(y_ref.dtype))

    direction(0, pf_ref, cf_ref, nf_ref, k > 0, k < nk - 1, yf_ref)
    direction(1, pb_ref, cb_ref, nb_ref, k < nk - 1, k > 0, yb_ref)

    @pl.when(k == nk - 1)
    def _():
        hfin_ref[...] = h_scr[...]


def _lru_call(y3, h0, conv_w, conv_b, wa, wi, ba, bi, lam):
    batch, t_seq, _ = y3.shape
    w = 1024
    tt = 64
    halo = BF16_ROWS
    per = tt // halo
    nk = t_seq // tt
    last = t_seq // halo - 1

    def specs(chunk):
        return [pl.BlockSpec((batch, halo, w), lambda s: (0, jnp.maximum(chunk(s) * per - 1, 0), COL_RX)),
                pl.BlockSpec((batch, tt, w), lambda s: (0, chunk(s), COL_RX)),
                pl.BlockSpec((batch, halo, w), lambda s: (0, jnp.minimum((chunk(s) + 1) * per, last), COL_RX))]

    fwd = lambda s: s
    bwd = lambda s: nk - 1 - s
    params = [conv_w, conv_b.reshape(1, w), wa, wi, ba.reshape(2, 1, w), bi.reshape(2, 1, w),
              lam.reshape(2, 1, w)]
    tm = tt * batch
    return pl.pallas_call(
        _lru_kernel,
        grid=(nk,),
        in_specs=specs(fwd) + specs(bwd) + [_full(h0, 1)] + [_full(a, 1) for a in params],
        out_specs=[pl.BlockSpec((batch, tt, w), lambda s: (0, fwd(s), 0)),
                   pl.BlockSpec((batch, tt, w), lambda s: (0, bwd(s), 0)),
                   pl.BlockSpec((2, batch, w), lambda s: (0, 0, 0))],
        out_shape=[jax.ShapeDtypeStruct((batch, t_seq, w), BF16)] * 2
                  + [jax.ShapeDtypeStruct((2, batch, w), F32)],
        scratch_shapes=[pltpu.VMEM((2, batch, w), F32),
                        pltpu.VMEM((RNN_BLOCKS, (tt + 2 * halo) * batch, w // RNN_BLOCKS), F32)]
                       + [pltpu.VMEM((RNN_BLOCKS, tm, w // RNN_BLOCKS), F32)] * 3,
        compiler_params=_cparams(1),
        name="conv_rglru_scan",
    )(y3, y3, y3, y3, y3, y3, h0, *params)


def _softmax_pv(q, kv_pieces, cols, vcols):
    scores = [lax.dot_general(q, k_ref[:, cols], (((1,), (1,)), ((), ())), preferred_element_type=F32)
              for k_ref, _ in kv_pieces]
    m = functools.reduce(jnp.maximum, [jnp.max(s, axis=-1, keepdims=True) for s in scores])
    es = [jnp.exp2(s - m) for s in scores]
    l = functools.reduce(jnp.add, [jnp.sum(e, axis=-1, keepdims=True) for e in es])
    o = functools.reduce(jnp.add, [jnp.dot(e.astype(BF16), v_ref[:, vcols], preferred_element_type=F32)
                                   for e, (_, v_ref) in zip(es, kv_pieces)])
    return o * (1.0 / l)


def _mla_attn_kernel(q_ref, *refs):
    o_ref = refs[-1]
    kv = [(refs[i], refs[i + 1]) for i in range(0, len(refs) - 1, 2)]
    lane = lax.broadcasted_iota(jnp.int32, (q_ref.shape[0], LANES), 1)
    for p in range(q_ref.shape[1] // (2 * LANES)):
        vcols = slice(p * LANES, (p + 1) * LANES)
        outs = [_softmax_pv(q_ref[:, hh * LANES:(hh + 1) * LANES], kv,
                            slice(hh * LANES, (hh + 1) * LANES), vcols) for hh in (2 * p, 2 * p + 1)]
        o_ref[:, vcols] = jnp.where(lane < MLA_V, outs[0], outs[1]).astype(o_ref.dtype)


def _diff_attn_kernel(dl_ref, q_ref, *refs, lam_init):
    o_ref = refs[-1]
    kv = [(refs[i], refs[i + 1]) for i in range(0, len(refs) - 1, 2)]
    lane = lax.broadcasted_iota(jnp.int32, (q_ref.shape[0], LANES), 1)
    dl = dl_ref[...]
    lam = (jnp.exp(jnp.sum(dl[0:1] * dl[1:2], axis=-1, keepdims=True))
           - jnp.exp(jnp.sum(dl[2:3] * dl[3:4], axis=-1, keepdims=True)) + lam_init)
    for h in range(q_ref.shape[1] // LANES):
        cols = slice(h * LANES, (h + 1) * LANES)
        q = q_ref[:, cols].astype(F32)
        o0 = _softmax_pv(jnp.where(lane < DIFF_HD, q, 0.0).astype(BF16), kv, cols, cols)
        o1 = _softmax_pv(jnp.where(lane >= DIFF_HD, q, 0.0).astype(BF16), kv, cols, cols)
        o_ref[:, cols] = (o0 - lam * o1).astype(o_ref.dtype)


def _attn_call(kernel, extra, q, kv_arrays, *, batch, t_q, tq, heads_per_step, qw, vw, n_heads, vcol0s, name):
    groups = n_heads // heads_per_step
    nq = t_q // tq
    kv_specs, kv_args = [], []
    for (k, v, t_kv), vcol0 in zip(kv_arrays, vcol0s):
        kv_specs += [pl.BlockSpec((t_kv, heads_per_step * qw), lambda b, g, i: (b, g)),
                     pl.BlockSpec((t_kv, heads_per_step * vw), lambda b, g, i, c=vcol0: (b, c + g))]
        kv_args += [k, v]
    return pl.pallas_call(
        kernel,
        grid=(batch, groups, nq),
        in_specs=[_full(a, 3) for a in extra]
                 + [pl.BlockSpec((tq, heads_per_step * qw), lambda b, g, i: (b * nq + i, g))] + kv_specs,
        out_specs=pl.BlockSpec((tq, heads_per_step * vw), lambda b, g, i: (b * nq + i, g)),
        out_shape=jax.ShapeDtypeStruct((batch * t_q, n_heads * vw), BF16),
        compiler_params=_cparams(3),
        name=name,
    )(*extra, q, *kv_args)


def _merge_kernel(x_ref, yf_ref, yb_ref, rg_ref, ob_ref, oc_ref, mg_ref, g1_ref, sub_g_ref,
                  wa_ref, wb_ref, wc_ref, wo_ref, o_ref, *, lam_init):
    d = x_ref.shape[-1]
    ya = yf_ref[...].astype(F32) + yb_ref[...].astype(F32)
    za = (ya * _gelu_tanh(rg_ref[...].astype(F32))).astype(BF16)
    br_a = jnp.dot(za, wa_ref[...], preferred_element_type=F32)
    br_b = jnp.dot(ob_ref[...], wb_ref[...], preferred_element_type=F32)
    oc = oc_ref[...].astype(F32)
    sub_gain = sub_g_ref[...] * (1.0 - lam_init)
    oc_n = jnp.concatenate(
        [_rms(oc[:, h * DIFF_V:(h + 1) * DIFF_V], DIFF_V) * sub_gain for h in range(DIFF_HEADS)], axis=1)
    br_c = jnp.dot(oc_n.astype(BF16), wc_ref[...], preferred_element_type=F32)
    mg = mg_ref[...].astype(F32)
    mix = (_sigmoid(mg[:, :d]) * br_a + _sigmoid(mg[:, d:2 * d]) * br_b
           + _sigmoid(mg[:, 2 * d:]) * br_c)
    m = jnp.dot(mix.astype(BF16), wo_ref[...], preferred_element_type=F32)
    o_ref[...] = x_ref[...] + g1_ref[...] * m


def _merge_call(xs, yf, yb, y, ob, oc, g1, sub_g, wa, wb, wc, wo, lam_init):
    rows, d = xs.shape
    tm = 256
    per_seq = rows // tm // g1.shape[0]
    blk = lambda: pl.BlockSpec((tm, d), lambda i: (i, 0))
    return pl.pallas_call(
        functools.partial(_merge_kernel, lam_init=lam_init),
        grid=(rows // tm,),
        in_specs=[blk(), blk(), blk(),
                  pl.BlockSpec((tm, d), lambda i: (i, COL_RG)),
                  blk(), blk(),
                  pl.BlockSpec((tm, N_BRANCH * d), lambda i: (i, COL_MG // N_BRANCH)),
                  pl.BlockSpec((None, 1, d), lambda i: (i // per_seq, 0, 0)),
                  _full(sub_g, 1), _full(wa, 1), _full(wb, 1), _full(wc, 1), _full(wo, 1)],
        out_specs=blk(),
        out_shape=jax.ShapeDtypeStruct((rows, d), F32),
        compiler_params=_cparams(1),
        name="branch_merge",
    )(xs, yf, yb, y, ob, oc, y, g1, sub_g, wa, wb, wc, wo)


def _ffn_kernel(x_ref, g_ref, sc_ref, sh_ref, g2_ref, win_ref, wout_ref, o_ref, h_scr, acc_scr):
    j = pl.program_id(1)

    @pl.when(j == 0)
    def _():
        xn = _rms(x_ref[...], x_ref.shape[-1]) * g_ref[...]
        h_scr[...] = (xn * (1.0 + sc_ref[...]) + sh_ref[...]).astype(BF16)
        acc_scr[...] = jnp.zeros_like(acc_scr)

    gu = jnp.dot(h_scr[...], win_ref[...], preferred_element_type=F32)
    th = gu.shape[1] // 2
    act = (_silu(gu[:, :th]) * gu[:, th:]).astype(BF16)
    acc_scr[...] += jnp.dot(act, wout_ref[...], preferred_element_type=F32)

    @pl.when(j == pl.num_programs(1) - 1)
    def _():
        o_ref[...] = x_ref[...] + g2_ref[...] * acc_scr[...]


def _ffn_call(xs, norm_g, sc, sh, g2, w_in_p, w_out, th):
    rows, d = xs.shape
    hidden = w_out.shape[0]
    tm = 512
    per_seq = rows // tm // sc.shape[0]
    seq = lambda i, j: (i // per_seq, 0, 0)
    return pl.pallas_call(
        _ffn_kernel,
        grid=(rows // tm, hidden // th),
        in_specs=[pl.BlockSpec((tm, d), lambda i, j: (i, 0)),
                  pl.BlockSpec((1, d), lambda i, j: (0, 0)),
                  pl.BlockSpec((None, 1, d), seq),
                  pl.BlockSpec((None, 1, d), seq),
                  pl.BlockSpec((None, 1, d), seq),
                  pl.BlockSpec((d, 2 * th), lambda i, j: (0, j)),
                  pl.BlockSpec((th, d), lambda i, j: (j, 0))],
        out_specs=pl.BlockSpec((tm, d), lambda i, j: (i, 0)),
        out_shape=jax.ShapeDtypeStruct((rows, d), F32),
        scratch_shapes=[pltpu.VMEM((tm, d), BF16), pltpu.VMEM((tm, d), F32)],
        compiler_params=_cparams(2),
        name="norm_swiglu_ffn",
    )(xs, norm_g.reshape(1, d), sc, sh, g2, w_in_p, w_out)


def _rope_lane_tables(t_lat, rot_dim, lane_of_pair, copies):
    rows_n = t_lat // GRID_W
    row_ids = jnp.repeat(jnp.arange(rows_n, dtype=F32), GRID_W)
    col_ids = jnp.tile(jnp.arange(GRID_W, dtype=F32), rows_n)
    n = rot_dim // 4
    freqs = ROPE_BASE ** (-jnp.arange(n, dtype=F32) / n)
    ang = jnp.concatenate([row_ids[:, None] * freqs, col_ids[:, None] * freqs], axis=-1)
    cos, sin = jnp.cos(ang), jnp.sin(ang)
    npairs = rot_dim // 2
    sel_c = np.zeros((npairs, LANES), np.float32)
    sel_s = np.zeros((npairs, LANES), np.float32)
    base_c = np.ones((LANES,), np.float32)
    for off in copies:
        for i in range(npairs):
            lane = off + lane_of_pair(i)
            sel_c[i, lane] = sel_c[i, lane + 1] = 1.0
            sel_s[i, lane] = -1.0
            sel_s[i, lane + 1] = 1.0
            base_c[lane] = base_c[lane + 1] = 0.0
    pick_c = np.argmax(sel_c, axis=0)
    pick_s = np.argmax(np.abs(sel_s), axis=0)
    c_tab = jnp.where(jnp.asarray(base_c > 0)[None, :], 1.0, cos[:, pick_c])
    s_tab = sin[:, pick_s] * jnp.asarray(sel_s.sum(axis=0))[None, :]
    return c_tab, s_tab


def _pair_swap_matrix(rot_dim, lane_of_pair, copies):
    p = np.zeros((LANES, LANES), np.float32)
    for off in copies:
        for i in range(rot_dim // 2):
            lane = off + lane_of_pair(i)
            p[lane + 1, lane] = 1.0
            p[lane, lane + 1] = 1.0
    return jnp.asarray(p, BF16)


def kernel(x, c, ctx, c_ctx, w_mod, b_mod, norm1_g, norm2_g, w_in, conv_w, conv_b, lru_wa, lru_ba, lru_wi, lru_bi, lru_lambda, mla_qn_g, mla_w_uq, mla_kvn_g, mla_w_ukv, mla_q_g, mla_k_g, diff_q_g, diff_k_g, diff_lambda, diff_subln_g, w_br_a, w_br_b, w_br_c, w_out, w_ffn_in, w_ffn_out):
    batch, t_lat, d = x.shape
    t_ctx = ctx.shape[1]
    depth = w_mod.shape[0]
    assert batch == SUBLANES and d == 1024
    q_rank = mla_qn_g.shape[1]
    kv_rank = mla_kvn_g.shape[1]
    hidden = w_ffn_out.shape[1]
    th = hidden // 2
    head_dim = MLA_NOPE + MLA_ROPE

    xl = x.reshape(batch * t_lat, d)
    xc = ctx.reshape(batch * t_ctx, d)

    cc = jnp.concatenate([c, c_ctx[None, :], jnp.zeros((BF16_ROWS - batch - 1, d), F32)], axis=0)
    mod = _mod_call(cc, w_mod, b_mod)

    mla_pair_lane = lambda i: MLA_NOPE + 2 * i
    diff_pair_lane = lambda i: 2 * i
    tabs = (_rope_lane_tables(t_lat, MLA_ROPE, mla_pair_lane, (0,))
            + _rope_lane_tables(t_lat, DIFF_HD, diff_pair_lane, (0, DIFF_HD)))
    perms = (_pair_swap_matrix(MLA_ROPE, mla_pair_lane, (0,)),
             _pair_swap_matrix(DIFF_HD, diff_pair_lane, (0, DIFF_HD)))
    ones_np = np.zeros((LANES, LANES), np.float32)
    ones_np[:DIFF_HD, :DIFF_HD] = 1.0
    ones_np[DIFF_HD:, DIFF_HD:] = 1.0
    ones_blk = jnp.asarray(ones_np, BF16)

    offs = np.cumsum([0, d, d, q_rank, kv_rank, MLA_ROPE, d, d, d, N_BRANCH * d])
    o_rx, o_rg, o_cq, o_ckv, o_kr, o_dq, o_dk, o_dv, o_mg = offs[:9]
    blk_w = d // RNN_BLOCKS

    for l in range(depth):
        need_ctx = l < depth - 1
        lam_init = 0.8 - 0.6 * math.exp(-0.3 * l)
        wl = w_in[l]
        mla_cols = jnp.concatenate(
            [wl[:, o_cq:o_cq + q_rank], wl[:, o_ckv:o_ckv + kv_rank],
             jnp.zeros((d, MLA_NOPE), F32), wl[:, o_kr:o_kr + MLA_ROPE],
             jnp.zeros((d, d - q_rank - kv_rank - head_dim), F32)], axis=1)
        w_in_p = jnp.concatenate(
            [mla_cols, wl[:, o_rx:o_rx + d], wl[:, o_rg:o_rg + d], wl[:, o_dq:o_dq + d],
             wl[:, o_dk:o_dk + d], wl[:, o_dv:o_dv + d], wl[:, o_mg:o_mg + N_BRANCH * d]],
            axis=1).astype(BF16)
        w_uq_p = jnp.pad(mla_w_uq[l].reshape(q_rank, MLA_HEADS, head_dim),
                         ((0, 0), (0, 0), (0, LANES - head_dim))).reshape(q_rank, MLA_HEADS * LANES).astype(BF16)
        w_ukv = mla_w_ukv[l].reshape(kv_rank, MLA_HEADS, MLA_NOPE + MLA_V)
        w_k_p = jnp.pad(w_ukv[:, :, :MLA_NOPE],
                        ((0, 0), (0, 0), (0, LANES - MLA_NOPE))).reshape(kv_rank, MLA_HEADS * LANES).astype(BF16)
        w_v_p = w_ukv[:, :, MLA_NOPE:].reshape(kv_rank, MLA_HEADS * MLA_V).astype(BF16)
        pad_gain = lambda g: jnp.pad(g, (0, LANES - head_dim)).reshape(1, LANES)
        gains = [mla_qn_g[l].reshape(1, q_rank), mla_kvn_g[l].reshape(1, kv_rank),
                 pad_gain(mla_q_g[l]), pad_gain(mla_k_g[l]),
                 jnp.tile(diff_q_g[l], 2).reshape(1, LANES), jnp.tile(diff_k_g[l], 2).reshape(1, LANES)]
        prep_w = (w_uq_p, w_k_p, w_v_p)
        ffn_in = w_ffn_in[l]
        w_ffn_in_p = jnp.concatenate(
            [ffn_in[:, j * th:(j + 1) * th] if half == 0 else ffn_in[:, hidden + j * th:hidden + (j + 1) * th]
             for j in range(hidden // th) for half in (0, 1)], axis=1).astype(BF16)
        w_ffn_out_p = w_ffn_out[l].astype(BF16)
        lru_params = (conv_w[l], conv_b[l],
                      lru_wa[l].reshape(2 * RNN_BLOCKS, blk_w, blk_w).astype(BF16),
                      lru_wi[l].reshape(2 * RNN_BLOCKS, blk_w, blk_w).astype(BF16),
                      lru_ba[l], lru_bi[l], lru_lambda[l])
        merge_w = (diff_subln_g[l].reshape(1, DIFF_V), w_br_a[l].astype(BF16), w_br_b[l].astype(BF16),
                   w_br_c[l].astype(BF16), w_out[l].astype(BF16))

        mod6 = mod[l].reshape(mod.shape[1], 6, d)
        lat_mod = [mod6[:batch, i].reshape(batch, 1, d) for i in range(6)]
        ctx_mod = [mod6[batch:batch + 1, i].reshape(1, 1, d) for i in range(6)]

        yc = _inproj_call(xc, norm1_g[l], ctx_mod[1], ctx_mod[0], w_in_p)
        yc3 = yc.reshape(batch, t_ctx, yc.shape[1])
        yfc, ybc, h_ctx = _lru_call(yc3, jnp.zeros((2, batch, d), F32), *lru_params)
        qmc, kmc, vmc, qdc, kdc = _prep_call(yc, (), gains, prep_w, (ones_blk,), t_ctx)

        yl = _inproj_call(xl, norm1_g[l], lat_mod[1], lat_mod[0], w_in_p)
        yl3 = yl.reshape(batch, t_lat, yl.shape[1])
        yfl, ybl, _ = _lru_call(yl3, h_ctx, *lru_params)
        qml, kml, vml, qdl, kdl = _prep_call(yl, tabs, gains, prep_w, (ones_blk,) + perms, t_lat)

        mla_kw = dict(batch=batch, heads_per_step=4, qw=LANES, vw=MLA_V, n_heads=MLA_HEADS)
        diff_kw = dict(batch=batch, heads_per_step=2, qw=LANES, vw=DIFF_V, n_heads=DIFF_HEADS)
        diff_kernel = functools.partial(_diff_attn_kernel, lam_init=lam_init)
        dv_col0 = COL_DV * (DIFF_HEADS // 2)
        tq = 512
        o_b = _attn_call(_mla_attn_kernel, (), qml, [(kml, vml, t_lat), (kmc, vmc, t_ctx)],
                         t_q=t_lat, tq=tq, vcol0s=(0, 0), name="mla_attention", **mla_kw)
        o_c = _attn_call(diff_kernel, (diff_lambda[l],), qdl, [(kdl, yl, t_lat), (kdc, yc, t_ctx)],
                         t_q=t_lat, tq=tq, vcol0s=(dv_col0, dv_col0), name="diff_attention", **diff_kw)
        xl_new = _merge_call(xl, yfl.reshape(-1, d), ybl.reshape(-1, d), yl, o_b, o_c, lat_mod[2],
                             *merge_w, lam_init)
        xl = _ffn_call(xl_new, norm2_g[l], lat_mod[4], lat_mod[3], lat_mod[5], w_ffn_in_p, w_ffn_out_p, th)

        if need_ctx:
            o_bc = _attn_call(_mla_attn_kernel, (), qmc, [(kmc, vmc, t_ctx)],
                              t_q=t_ctx, tq=t_ctx, vcol0s=(0,), name="mla_attention_ctx", **mla_kw)
            o_cc = _attn_call(diff_kernel, (diff_lambda[l],), qdc, [(kdc, yc, t_ctx)],
                              t_q=t_ctx, tq=t_ctx, vcol0s=(dv_col0,), name="diff_attention_ctx", **diff_kw)
            xc_new = _merge_call(xc, yfc.reshape(-1, d), ybc.reshape(-1, d), yc, o_bc, o_cc, ctx_mod[2],
                                 *merge_w, lam_init)
            xc = _ffn_call(xc_new, norm2_g[l], ctx_mod[4], ctx_mod[3], ctx_mod[5], w_ffn_in_p, w_ffn_out_p, th)

    return xl.reshape(batch, t_lat, d)
```

```python
import functools
import math

import numpy as np
import jax
import jax.numpy as jnp
from jax import lax
from jax.experimental import pallas as pl
from jax.experimental.pallas import tpu as pltpu

F32 = jnp.float32
BF16 = jnp.bfloat16

EPS = 1e-6
GRID_W = 64
ROPE_BASE = 10000.0
LRU_C = 8.0
CONV_W = 4
RNN_BLOCKS = 8
MLA_HEADS = 16
MLA_NOPE = 64
MLA_ROPE = 32
MLA_V = 64
MLA_SCALE = (MLA_NOPE + MLA_ROPE) ** -0.5
DIFF_HEADS = 8
DIFF_HD = 64
DIFF_V = 2 * DIFF_HD
DIFF_SCALE = DIFF_HD ** -0.5
N_BRANCH = 3
LOG2E = math.log2(math.e)

LANES = 128
SUBLANES = 8
BF16_ROWS = 16
VMEM_LIMIT = 56 * 1024 * 1024

COL_MLA, COL_RX, COL_RG, COL_DQ, COL_DK, COL_DV, COL_MG = 0, 1, 2, 3, 4, 5, 6


def _cparams(n_axes):
    return pltpu.CompilerParams(dimension_semantics=("arbitrary",) * n_axes,
                                vmem_limit_bytes=VMEM_LIMIT)


def _full(a, n_axes):
    return pl.BlockSpec(a.shape, lambda *_: (0,) * a.ndim)


def _sigmoid(x):
    return 0.5 * jnp.tanh(0.5 * x) + 0.5


def _silu(x):
    return x * _sigmoid(x)


def _gelu_tanh(x):
    return 0.5 * x * (1.0 + jnp.tanh(math.sqrt(2.0 / math.pi) * (x + 0.044715 * (x * x * x))))


def _rms(x, denom):
    ms = jnp.sum(x * x, axis=-1, keepdims=True) * (1.0 / denom)
    return x * lax.rsqrt(ms + EPS)


def _mod_kernel(c_ref, w_ref, b_ref, o_ref):
    s = _silu(c_ref[...]).astype(BF16)
    o_ref[...] = jnp.dot(s, w_ref[...].astype(BF16), preferred_element_type=F32) + b_ref[...]


def _mod_call(cc, w_mod, b_mod):
    depth, d, n = w_mod.shape
    tn = 1024
    return pl.pallas_call(
        _mod_kernel,
        grid=(depth, n // tn),
        in_specs=[pl.BlockSpec((cc.shape[0], d), lambda l, j: (0, 0)),
                  pl.BlockSpec((None, d, tn), lambda l, j: (l, 0, j)),
                  pl.BlockSpec((None, 1, tn), lambda l, j: (l, 0, j))],
        out_specs=pl.BlockSpec((None, cc.shape[0], tn), lambda l, j: (l, 0, j)),
        out_shape=jax.ShapeDtypeStruct((depth, cc.shape[0], n), F32),
        compiler_params=_cparams(2),
        name="adaln_mod",
    )(cc, w_mod, b_mod.reshape(depth, 1, n))


def _inproj_kernel(x_ref, g_ref, sc_ref, sh_ref, w_ref, o_ref, h_scr):
    @pl.when(pl.program_id(1) == 0)
    def _():
        xn = _rms(x_ref[...], x_ref.shape[-1]) * g_ref[...]
        h_scr[...] = (xn * (1.0 + sc_ref[...]) + sh_ref[...]).astype(BF16)

    o_ref[...] = jnp.dot(h_scr[...], w_ref[...], preferred_element_type=F32).astype(o_ref.dtype)


def _inproj_call(xs, norm_g, sc, sh, w_in_p):
    rows, d = xs.shape
    n = w_in_p.shape[1]
    tm, tn = min(1024, rows // sc.shape[0]), 1024
    per_seq = rows // tm // sc.shape[0]
    seq = lambda i, j: (i // per_seq, 0, 0)
    return pl.pallas_call(
        _inproj_kernel,
        grid=(rows // tm, n // tn),
        in_specs=[pl.BlockSpec((tm, d), lambda i, j: (i, 0)),
                  pl.BlockSpec((1, d), lambda i, j: (0, 0)),
                  pl.BlockSpec((None, 1, d), seq),
                  pl.BlockSpec((None, 1, d), seq),
                  pl.BlockSpec((d, tn), lambda i, j: (0, j))],
        out_specs=pl.BlockSpec((tm, tn), lambda i, j: (i, j)),
        out_shape=jax.ShapeDtypeStruct((rows, n), BF16),
        scratch_shapes=[pltpu.VMEM((tm, d), BF16)],
        compiler_params=_cparams(2),
        name="norm_inproj",
    )(xs, norm_g.reshape(1, d), sc, sh, w_in_p)


def _prep_kernel(*refs, use_rope):
    if use_rope:
        (ymla_ref, ydq_ref, ydk_ref, cm_ref, sm_ref, cd_ref, sd_ref,
         qn_g_ref, kvn_g_ref, q_g_ref, k_g_ref, dq_g_ref, dk_g_ref,
         wuq_ref, wk_ref, wv_ref, ones_ref, pm_ref, pd_ref,
         qm_ref, km_ref, vm_ref, qd_ref, kd_ref) = refs
        cm, sm, cd, sd = cm_ref[...], sm_ref[...], cd_ref[...], sd_ref[...]
        pm, pd = pm_ref[...], pd_ref[...]
    else:
        (ymla_ref, ydq_ref, ydk_ref,
         qn_g_ref, kvn_g_ref, q_g_ref, k_g_ref, dq_g_ref, dk_g_ref,
         wuq_ref, wk_ref, wv_ref, ones_ref,
         qm_ref, km_ref, vm_ref, qd_ref, kd_ref) = refs
        cm = sm = cd = sd = pm = pd = None
    q_rank = qn_g_ref.shape[-1]
    kv_rank = kvn_g_ref.shape[-1]
    ymla = ymla_ref[...].astype(F32)
    cq = ymla[:, :q_rank]
    ckv = ymla[:, q_rank:q_rank + kv_rank]
    kr = ymla[:, q_rank + kv_rank:q_rank + kv_rank + LANES]
    head_dim = MLA_NOPE + MLA_ROPE

    def rope(u, cos, sin, perm):
        if not use_rope:
            return u
        return u * cos + jnp.dot(u.astype(BF16), perm, preferred_element_type=F32) * sin

    cqn = (_rms(cq, q_rank) * qn_g_ref[...]).astype(BF16)
    q = jnp.dot(cqn, wuq_ref[...], preferred_element_type=F32)
    q_gain = q_g_ref[...] * (MLA_SCALE * LOG2E)
    for h in range(MLA_HEADS):
        u = _rms(q[:, h * LANES:(h + 1) * LANES], head_dim) * q_gain
        qm_ref[:, h * LANES:(h + 1) * LANES] = rope(u, cm, sm, pm).astype(BF16)

    ckvn = (_rms(ckv, kv_rank) * kvn_g_ref[...]).astype(BF16)
    vm_ref[...] = jnp.dot(ckvn, wv_ref[...], preferred_element_type=F32).astype(BF16)
    kn = jnp.dot(ckvn, wk_ref[...], preferred_element_type=F32)
    k_gain = k_g_ref[...]
    for h in range(MLA_HEADS):
        u = _rms(kn[:, h * LANES:(h + 1) * LANES] + kr, head_dim) * k_gain
        km_ref[:, h * LANES:(h + 1) * LANES] = rope(u, cm, sm, pm).astype(BF16)

    ones_blk = ones_ref[...]

    def diff_norm_rope(y_ref, gain, o_ref):
        for h in range(DIFF_HEADS):
            x = y_ref[:, h * LANES:(h + 1) * LANES].astype(F32)
            ms = jnp.dot((x * x).astype(BF16), ones_blk, preferred_element_type=F32) * (1.0 / DIFF_HD)
            u = x * lax.rsqrt(ms + EPS) * gain
            o_ref[:, h * LANES:(h + 1) * LANES] = rope(u, cd, sd, pd).astype(BF16)

    diff_norm_rope(ydq_ref, dq_g_ref[...] * (DIFF_SCALE * LOG2E), qd_ref)
    diff_norm_rope(ydk_ref, dk_g_ref[...], kd_ref)


def _prep_call(y, tabs, gains, weights, consts, t_seq):
    rows = y.shape[0]
    d = 1024
    tm = 256
    use_rope = len(tabs) > 0
    per_seq = t_seq // tm
    row_blk = lambda w: pl.BlockSpec((tm, w), lambda i: (i, 0))
    col_blk = lambda c: pl.BlockSpec((tm, d), lambda i, c=c: (i, c))
    tab_blk = pl.BlockSpec((tm, LANES), lambda i: (i % per_seq, 0))
    wq = MLA_HEADS * LANES
    out_shapes = [jax.ShapeDtypeStruct((rows, wq), BF16), jax.ShapeDtypeStruct((rows, wq), BF16),
                  jax.ShapeDtypeStruct((rows, MLA_HEADS * MLA_V), BF16),
                  jax.ShapeDtypeStruct((rows, d), BF16), jax.ShapeDtypeStruct((rows, d), BF16)]
    small = list(gains) + list(weights) + list(consts)
    return pl.pallas_call(
        functools.partial(_prep_kernel, use_rope=use_rope),
        grid=(rows // tm,),
        in_specs=[col_blk(COL_MLA), col_blk(COL_DQ), col_blk(COL_DK)]
                 + [tab_blk] * len(tabs) + [_full(a, 1) for a in small],
        out_specs=[row_blk(s.shape[1]) for s in out_shapes],
        out_shape=out_shapes,
        compiler_params=_cparams(1),
        name="attn_prep_rope" if use_rope else "attn_prep",
    )(y, y, y, *tabs, *small)


def _lru_kernel(pf_ref, cf_ref, nf_ref, pb_ref, cb_ref, nb_ref, h0_ref,
                cw_ref, cbias_ref, wa_ref, wi_ref, ba_ref, bi_ref, lam_ref,
                yf_ref, yb_ref, hfin_ref, h_scr, win_scr, a_scr, b_scr, y_scr):
    k = pl.program_id(0)
    nk = pl.num_programs(0)
    batch, tt, w = cf_ref.shape
    halo = pf_ref.shape[1]
    tm = tt * batch
    blk_w = w // RNN_BLOCKS

    @pl.when(k == 0)
    def _():
        h_scr[...] = h0_ref[...]

    def interleave(src_ref, row0, n_steps, keep):
        for b in range(batch):
            v = src_ref[b].astype(F32)
            if keep is not None:
                v = jnp.where(keep, v, 0.0)
            for n in range(RNN_BLOCKS):
                win_scr[n, pl.ds(row0 + b, n_steps, stride=batch), :] = v[:, n * blk_w:(n + 1) * blk_w]

    def direction(r, prev_ref, cur_ref, next_ref, has_prev, has_next, y_ref):
        interleave(prev_ref, 0, halo, has_prev)
        interleave(cur_ref, halo * batch, tt, None)
        interleave(next_ref, (halo + tt) * batch, halo, has_next)
        base = halo * batch
        for n in range(RNN_BLOCKS):
            lanes = slice(n * blk_w, (n + 1) * blk_w)
            xc = cbias_ref[:, lanes]
            for j in range(CONV_W):
                off = base + (j - CONV_W // 2) * batch
                xc = xc + win_scr[n, off:off + tm, :] * cw_ref[j:j + 1, lanes]
            xb = xc.astype(BF16)
            gr = jnp.dot(xb, wa_ref[r * RNN_BLOCKS + n], preferred_element_type=F32) + ba_ref[r][:, lanes]
            gi = jnp.dot(xb, wi_ref[r * RNN_BLOCKS + n], preferred_element_type=F32) + bi_ref[r][:, lanes]
            lam = lam_ref[r][:, lanes]
            softplus_neg = jnp.maximum(-lam, 0.0) + jnp.log(1.0 + jnp.exp(-jnp.abs(lam)))
            a = jnp.exp2(_sigmoid(gr) * (softplus_neg * (-LRU_C * LOG2E)))
            a_scr[n] = a
            b_scr[n] = jnp.sqrt(1.0 - a * a) * _sigmoid(gi) * xc
        h = [h_scr[r, :, n * blk_w:(n + 1) * blk_w] for n in range(RNN_BLOCKS)]
        order = range(tt) if r == 0 else range(tt - 1, -1, -1)
        for j in order:
            rows = pl.ds(j * batch, batch)
            for n in range(RNN_BLOCKS):
                h[n] = a_scr[n, rows, :] * h[n] + b_scr[n, rows, :]
                y_scr[n, rows, :] = h[n]
        for n in range(RNN_BLOCKS):
            h_scr[r, :, n * blk_w:(n + 1) * blk_w] = h[n]
        for b in range(batch):
            for n in range(RNN_BLOCKS):
                y_ref[b, :, n * blk_w:(n + 1) * blk_w] = (
                    y_scr[n, pl.ds(b, tt, stride=batch), :].astype(y_ref.dtype))

    direction(0, pf_ref, cf_ref, nf_ref, k > 0, k < nk - 1, yf_ref)
    direction(1, pb_ref, cb_ref, nb_ref, k < nk - 1, k > 0, yb_ref)

    @pl.when(k == nk - 1)
    def _():
        hfin_ref[...] = h_scr[...]


def _lru_call(y3, h0, conv_w, conv_b, wa, wi, ba, bi, lam):
    batch, t_seq, _ = y3.shape
    w = 1024
    tt = 64
    halo = BF16_ROWS
    per = tt // halo
    nk = t_seq // tt
    last = t_seq // halo - 1

    def specs(chunk):
        return [pl.BlockSpec((batch, halo, w), lambda s: (0, jnp.maximum(chunk(s) * per - 1, 0), COL_RX)),
                pl.BlockSpec((batch, tt, w), lambda s: (0, chunk(s), COL_RX)),
                pl.BlockSpec((batch, halo, w), lambda s: (0, jnp.minimum((chunk(s) + 1) * per, last), COL_RX))]

    fwd = lambda s: s
    bwd = lambda s: nk - 1 - s
    params = [conv_w, conv_b.reshape(1, w), wa, wi, ba.reshape(2, 1, w), bi.reshape(2, 1, w),
              lam.reshape(2, 1, w)]
    tm = tt * batch
    return pl.pallas_call(
        _lru_kernel,
        grid=(nk,),
        in_specs=specs(fwd) + specs(bwd) + [_full(h0, 1)] + [_full(a, 1) for a in params],
        out_specs=[pl.BlockSpec((batch, tt, w), lambda s: (0, fwd(s), 0)),
                   pl.BlockSpec((batch, tt, w), lambda s: (0, bwd(s), 0)),
                   pl.BlockSpec((2, batch, w), lambda s: (0, 0, 0))],
        out_shape=[jax.ShapeDtypeStruct((batch, t_seq, w), BF16)] * 2
                  + [jax.ShapeDtypeStruct((2, batch, w), F32)],
        scratch_shapes=[pltpu.VMEM((2, batch, w), F32),
                        pltpu.VMEM((RNN_BLOCKS, (tt + 2 * halo) * batch, w // RNN_BLOCKS), F32)]
                       + [pltpu.VMEM((RNN_BLOCKS, tm, w // RNN_BLOCKS), F32)] * 3,
        compiler_params=_cparams(1),
        name="conv_rglru_scan",
    )(y3, y3, y3, y3, y3, y3, h0, *params)


KEY_CHUNK = 512
ROW_STRIP = 64


def _key_chunks(kv_pieces):
    chunks = []
    col = 0
    for k_ref, v_ref in kv_pieces:
        n = k_ref.shape[0]
        kc = min(KEY_CHUNK, n)
        for r0 in range(0, n, kc):
            chunks.append((k_ref, v_ref, r0, col, kc))
            col += kc
    return chunks


def _lane_tiles(c0, kc):
    return [slice(c0 + i * LANES, c0 + (i + 1) * LANES) for i in range(kc // LANES)]


def _row_strips(tq):
    return [slice(r * ROW_STRIP, (r + 1) * ROW_STRIP) for r in range(tq // ROW_STRIP)]


def _pv_chunk(p_scr, chunk, vcols):
    _, v_ref, r0, c0, kc = chunk
    return jnp.dot(p_scr[:, c0:c0 + kc], v_ref[r0:r0 + kc, vcols], preferred_element_type=F32)


def _score_exp(q, chunks, cols, s_scr, p_scr, after_chunk=None):
    tq = q.shape[0]
    mpart = None
    for k_ref, _, r0, c0, kc in chunks:
        s = lax.dot_general(q, k_ref[r0:r0 + kc, cols], (((1,), (1,)), ((), ())), preferred_element_type=F32)
        s_scr[:, c0:c0 + kc] = s
        for i in range(kc // LANES):
            t = s[:, i * LANES:(i + 1) * LANES]
            mpart = t if mpart is None else jnp.maximum(mpart, t)
    mb = jnp.broadcast_to(jnp.max(mpart, axis=-1, keepdims=True), (tq, LANES))

    strips = _row_strips(tq)
    lparts = [None] * len(strips)
    for chunk in chunks:
        for r, rows in enumerate(strips):
            m_r = mb[rows]
            for lanes in _lane_tiles(chunk[3], chunk[4]):
                e = jnp.exp2(s_scr[rows, lanes] - m_r)
                lparts[r] = e if lparts[r] is None else lparts[r] + e
                p_scr[rows, lanes] = e.astype(BF16)
        if after_chunk is not None:
            after_chunk(chunk)
    return jnp.sum(jnp.concatenate(lparts, axis=0), axis=-1, keepdims=True)


def _softmax_pv(q, kv_pieces, cols, vcols, s_scr, p_scr):
    acc = []

    def pv(chunk):
        acc.append(_pv_chunk(p_scr, chunk, vcols))

    l = _score_exp(q, _key_chunks(kv_pieces), cols, s_scr, p_scr, after_chunk=pv)
    return functools.reduce(jnp.add, acc) * (1.0 / l)


class _UnitRunner:
    def __init__(self, n_units, tq, kv, s_scr, p_scr):
        self.n_units, self.tq, self.kv, self.s_scr, self.p_scr = n_units, tq, kv, s_scr, p_scr
        self.done = 0
        self.pieces = 0

    def run(self, q_of_rows, cols, vcols):
        edge = self.done in (0, self.n_units - 1)
        self.done += 1
        half = self.tq // 2
        if edge and half % ROW_STRIP == 0:
            row_sets = [pl.ds(0, half), pl.ds(half, half)]
        else:
            row_sets = [pl.ds(0, self.tq)]
        outs = []
        for rows in row_sets:
            buf = self.pieces % 2
            self.pieces += 1
            outs.append(_softmax_pv(q_of_rows(rows), self.kv, cols, vcols,
                                    self.s_scr.at[buf, rows], self.p_scr.at[buf, rows]))
        return outs[0] if len(outs) == 1 else jnp.concatenate(outs, axis=0)


def _mla_attn_kernel(q_ref, *refs):
    o_ref, s_scr, p_scr = refs[-3:]
    kv = [(refs[i], refs[i + 1]) for i in range(0, len(refs) - 3, 2)]
    tq = q_ref.shape[0]
    n_heads = q_ref.shape[1] // LANES
    lane = lax.broadcasted_iota(jnp.int32, (tq, LANES), 1)
    units = _UnitRunner(n_heads, tq, kv, s_scr, p_scr)
    for p in range(n_heads // 2):
        vcols = slice(p * LANES, (p + 1) * LANES)
        outs = []
        for hh in (2 * p, 2 * p + 1):
            cols = slice(hh * LANES, (hh + 1) * LANES)
            outs.append(units.run(lambda rows, cols=cols: q_ref[rows, cols], cols, vcols))
        o_ref[:, vcols] = jnp.where(lane < MLA_V, outs[0], outs[1]).astype(o_ref.dtype)


def _diff_attn_kernel(dl_ref, q_ref, *refs, lam_init):
    o_ref, s_scr, p_scr = refs[-3:]
    kv = [(refs[i], refs[i + 1]) for i in range(0, len(refs) - 3, 2)]
    lane = lax.broadcasted_iota(jnp.int32, (q_ref.shape[0], LANES), 1)
    dl = dl_ref[...]
    lam = (jnp.exp(jnp.sum(dl[0:1] * dl[1:2], axis=-1, keepdims=True))
           - jnp.exp(jnp.sum(dl[2:3] * dl[3:4], axis=-1, keepdims=True)) + lam_init)
    n_heads = q_ref.shape[1] // LANES
    units = _UnitRunner(2 * n_heads, q_ref.shape[0], kv, s_scr, p_scr)
    for h in range(n_heads):
        cols = slice(h * LANES, (h + 1) * LANES)

        def map_q(rows, first, cols=cols):
            q = q_ref[rows, cols].astype(F32)
            in_first = lax.broadcasted_iota(jnp.int32, q.shape, 1) < DIFF_HD
            return (jnp.where(in_first, q, 0.0) if first else jnp.where(in_first, 0.0, q)).astype(BF16)

        o0 = units.run(functools.partial(map_q, first=True), cols, cols)
        o1 = units.run(functools.partial(map_q, first=False), cols, cols)
        o_ref[:, cols] = (o0 - lam * o1).astype(o_ref.dtype)


def _attn_call(kernel, extra, q, kv_arrays, *, batch, t_q, tq, heads_per_step, qw, vw, n_heads, vcol0s,
               n_pbuf, name):
    groups = n_heads // heads_per_step
    nq = t_q // tq
    n_keys = sum(t_kv for _, _, t_kv in kv_arrays)
    kv_specs, kv_args = [], []
    for (k, v, t_kv), vcol0 in zip(kv_arrays, vcol0s):
        kv_specs += [pl.BlockSpec((t_kv, heads_per_step * qw), lambda b, g, i: (b, g)),
                     pl.BlockSpec((t_kv, heads_per_step * vw), lambda b, g, i, c=vcol0: (b, c + g))]
        kv_args += [k, v]
    return pl.pallas_call(
        kernel,
        grid=(batch, groups, nq),
        in_specs=[_full(a, 3) for a in extra]
                 + [pl.BlockSpec((tq, heads_per_step * qw), lambda b, g, i: (b * nq + i, g))] + kv_specs,
        out_specs=pl.BlockSpec((tq, heads_per_step * vw), lambda b, g, i: (b * nq + i, g)),
        out_shape=jax.ShapeDtypeStruct((batch * t_q, n_heads * vw), BF16),
        scratch_shapes=[pltpu.VMEM((2, tq, n_keys), F32), pltpu.VMEM((n_pbuf, tq, n_keys), BF16)],
        compiler_params=_cparams(3),
        name=name,
    )(*extra, q, *kv_args)


def _merge_kernel(x_ref, yf_ref, yb_ref, rg_ref, ob_ref, oc_ref, mg_ref, g1_ref, sub_g_ref,
                  wa_ref, wb_ref, wc_ref, wo_ref, o_ref, *, lam_init):
    d = x_ref.shape[-1]
    ya = yf_ref[...].astype(F32) + yb_ref[...].astype(F32)
    za = (ya * _gelu_tanh(rg_ref[...].astype(F32))).astype(BF16)
    br_a = jnp.dot(za, wa_ref[...], preferred_element_type=F32)
    br_b = jnp.dot(ob_ref[...], wb_ref[...], preferred_element_type=F32)
    oc = oc_ref[...].astype(F32)
    sub_gain = sub_g_ref[...] * (1.0 - lam_init)
    oc_n = jnp.concatenate(
        [_rms(oc[:, h * DIFF_V:(h + 1) * DIFF_V], DIFF_V) * sub_gain for h in range(DIFF_HEADS)], axis=1)
    br_c = jnp.dot(oc_n.astype(BF16), wc_ref[...], preferred_element_type=F32)
    mg = mg_ref[...].astype(F32)
    mix = (_sigmoid(mg[:, :d]) * br_a + _sigmoid(mg[:, d:2 * d]) * br_b
           + _sigmoid(mg[:, 2 * d:]) * br_c)
    m = jnp.dot(mix.astype(BF16), wo_ref[...], preferred_element_type=F32)
    o_ref[...] = x_ref[...] + g1_ref[...] * m


def _merge_call(xs, yf, yb, y, ob, oc, g1, sub_g, wa, wb, wc, wo, lam_init):
    rows, d = xs.shape
    tm = 256
    per_seq = rows // tm // g1.shape[0]
    blk = lambda: pl.BlockSpec((tm, d), lambda i: (i, 0))
    return pl.pallas_call(
        functools.partial(_merge_kernel, lam_init=lam_init),
        grid=(rows // tm,),
        in_specs=[blk(), blk(), blk(),
                  pl.BlockSpec((tm, d), lambda i: (i, COL_RG)),
                  blk(), blk(),
                  pl.BlockSpec((tm, N_BRANCH * d), lambda i: (i, COL_MG // N_BRANCH)),
                  pl.BlockSpec((None, 1, d), lambda i: (i // per_seq, 0, 0)),
                  _full(sub_g, 1), _full(wa, 1), _full(wb, 1), _full(wc, 1), _full(wo, 1)],
        out_specs=blk(),
        out_shape=jax.ShapeDtypeStruct((rows, d), F32),
        compiler_params=_cparams(1),
        name="branch_merge",
    )(xs, yf, yb, y, ob, oc, y, g1, sub_g, wa, wb, wc, wo)


def _ffn_kernel(x_ref, g_ref, sc_ref, sh_ref, g2_ref, wg_ref, wu_ref, wout_ref, o_ref, h_scr, acc_scr):
    j = pl.program_id(1)

    @pl.when(j == 0)
    def _():
        xn = _rms(x_ref[...], x_ref.shape[-1]) * g_ref[...]
        h_scr[...] = (xn * (1.0 + sc_ref[...]) + sh_ref[...]).astype(BF16)
        acc_scr[...] = jnp.zeros_like(acc_scr)

    h = h_scr[...]
    gate = jnp.dot(h, wg_ref[...], preferred_element_type=F32)
    up = jnp.dot(h, wu_ref[...], preferred_element_type=F32)
    act = (_silu(gate) * up).astype(BF16)
    acc_scr[...] += jnp.dot(act, wout_ref[...], preferred_element_type=F32)

    @pl.when(j == pl.num_programs(1) - 1)
    def _():
        o_ref[...] = x_ref[...] + g2_ref[...] * acc_scr[...]


def _ffn_call(xs, norm_g, sc, sh, g2, w_in, w_out, th):
    rows, d = xs.shape
    hidden = w_out.shape[0]
    nj = hidden // th
    tm = 512
    per_seq = rows // tm // sc.shape[0]
    seq = lambda i, j: (i // per_seq, 0, 0)
    return pl.pallas_call(
        _ffn_kernel,
        grid=(rows // tm, nj),
        in_specs=[pl.BlockSpec((tm, d), lambda i, j: (i, 0)),
                  pl.BlockSpec((1, d), lambda i, j: (0, 0)),
                  pl.BlockSpec((None, 1, d), seq),
                  pl.BlockSpec((None, 1, d), seq),
                  pl.BlockSpec((None, 1, d), seq),
                  pl.BlockSpec((d, th), lambda i, j: (0, j)),
                  pl.BlockSpec((d, th), lambda i, j: (0, nj + j)),
                  pl.BlockSpec((th, d), lambda i, j: (j, 0))],
        out_specs=pl.BlockSpec((tm, d), lambda i, j: (i, 0)),
        out_shape=jax.ShapeDtypeStruct((rows, d), F32),
        scratch_shapes=[pltpu.VMEM((tm, d), BF16), pltpu.VMEM((tm, d), F32)],
        compiler_params=_cparams(2),
        name="norm_swiglu_ffn",
    )(xs, norm_g.reshape(1, d), sc, sh, g2, w_in, w_in, w_out)


def _rope_lane_tables(t_lat, rot_dim, lane_of_pair, copies):
    rows_n = t_lat // GRID_W
    row_ids = jnp.repeat(jnp.arange(rows_n, dtype=F32), GRID_W)
    col_ids = jnp.tile(jnp.arange(GRID_W, dtype=F32), rows_n)
    n = rot_dim // 4
    freqs = ROPE_BASE ** (-jnp.arange(n, dtype=F32) / n)
    ang = jnp.concatenate([row_ids[:, None] * freqs, col_ids[:, None] * freqs], axis=-1)
    cos, sin = jnp.cos(ang), jnp.sin(ang)
    npairs = rot_dim // 2
    sel_c = np.zeros((npairs, LANES), np.float32)
    sel_s = np.zeros((npairs, LANES), np.float32)
    base_c = np.ones((LANES,), np.float32)
    for off in copies:
        for i in range(npairs):
            lane = off + lane_of_pair(i)
            sel_c[i, lane] = sel_c[i, lane + 1] = 1.0
            sel_s[i, lane] = -1.0
            sel_s[i, lane + 1] = 1.0
            base_c[lane] = base_c[lane + 1] = 0.0
    pick_c = np.argmax(sel_c, axis=0)
    pick_s = np.argmax(np.abs(sel_s), axis=0)
    c_tab = jnp.where(jnp.asarray(base_c > 0)[None, :], 1.0, cos[:, pick_c])
    s_tab = sin[:, pick_s] * jnp.asarray(sel_s.sum(axis=0))[None, :]
    return c_tab, s_tab


def _pair_swap_matrix(rot_dim, lane_of_pair, copies):
    p = np.zeros((LANES, LANES), np.float32)
    for off in copies:
        for i in range(rot_dim // 2):
            lane = off + lane_of_pair(i)
            p[lane + 1, lane] = 1.0
            p[lane, lane + 1] = 1.0
    return jnp.asarray(p, BF16)


def kernel(x, c, ctx, c_ctx, w_mod, b_mod, norm1_g, norm2_g, w_in, conv_w, conv_b, lru_wa, lru_ba, lru_wi, lru_bi, lru_lambda, mla_qn_g, mla_w_uq, mla_kvn_g, mla_w_ukv, mla_q_g, mla_k_g, diff_q_g, diff_k_g, diff_lambda, diff_subln_g, w_br_a, w_br_b, w_br_c, w_out, w_ffn_in, w_ffn_out):
    batch, t_lat, d = x.shape
    t_ctx = ctx.shape[1]
    depth = w_mod.shape[0]
    assert batch == SUBLANES and d == 1024
    q_rank = mla_qn_g.shape[1]
    kv_rank = mla_kvn_g.shape[1]
    hidden = w_ffn_out.shape[1]
    th = hidden // 2
    head_dim = MLA_NOPE + MLA_ROPE

    xl = x.reshape(batch * t_lat, d)
    xc = ctx.reshape(batch * t_ctx, d)

    cc = jnp.concatenate([c, c_ctx[None, :], jnp.zeros((BF16_ROWS - batch - 1, d), F32)], axis=0)
    mod = _mod_call(cc, w_mod, b_mod)

    mla_pair_lane = lambda i: MLA_NOPE + 2 * i
    diff_pair_lane = lambda i: 2 * i
    tabs = (_rope_lane_tables(t_lat, MLA_ROPE, mla_pair_lane, (0,))
            + _rope_lane_tables(t_lat, DIFF_HD, diff_pair_lane, (0, DIFF_HD)))
    perms = (_pair_swap_matrix(MLA_ROPE, mla_pair_lane, (0,)),
             _pair_swap_matrix(DIFF_HD, diff_pair_lane, (0, DIFF_HD)))
    ones_np = np.zeros((LANES, LANES), np.float32)
    ones_np[:DIFF_HD, :DIFF_HD] = 1.0
    ones_np[DIFF_HD:, DIFF_HD:] = 1.0
    ones_blk = jnp.asarray(ones_np, BF16)

    offs = np.cumsum([0, d, d, q_rank, kv_rank, MLA_ROPE, d, d, d, N_BRANCH * d])
    o_rx, o_rg, o_cq, o_ckv, o_kr, o_dq, o_dk, o_dv, o_mg = offs[:9]
    blk_w = d // RNN_BLOCKS

    for l in range(depth):
        need_ctx = l < depth - 1
        lam_init = 0.8 - 0.6 * math.exp(-0.3 * l)
        wl = w_in[l]
        mla_cols = jnp.concatenate(
            [wl[:, o_cq:o_cq + q_rank], wl[:, o_ckv:o_ckv + kv_rank],
             jnp.zeros((d, MLA_NOPE), F32), wl[:, o_kr:o_kr + MLA_ROPE],
             jnp.zeros((d, d - q_rank - kv_rank - head_dim), F32)], axis=1)
        w_in_p = jnp.concatenate(
            [mla_cols, wl[:, o_rx:o_rx + d], wl[:, o_rg:o_rg + d], wl[:, o_dq:o_dq + d],
             wl[:, o_dk:o_dk + d], wl[:, o_dv:o_dv + d], wl[:, o_mg:o_mg + N_BRANCH * d]],
            axis=1).astype(BF16)
        w_uq_p = jnp.pad(mla_w_uq[l].reshape(q_rank, MLA_HEADS, head_dim),
                         ((0, 0), (0, 0), (0, LANES - head_dim))).reshape(q_rank, MLA_HEADS * LANES).astype(BF16)
        w_ukv = mla_w_ukv[l].reshape(kv_rank, MLA_HEADS, MLA_NOPE + MLA_V)
        w_k_p = jnp.pad(w_ukv[:, :, :MLA_NOPE],
                        ((0, 0), (0, 0), (0, LANES - MLA_NOPE))).reshape(kv_rank, MLA_HEADS * LANES).astype(BF16)
        w_v_p = w_ukv[:, :, MLA_NOPE:].reshape(kv_rank, MLA_HEADS * MLA_V).astype(BF16)
        pad_gain = lambda g: jnp.pad(g, (0, LANES - head_dim)).reshape(1, LANES)
        gains = [mla_qn_g[l].reshape(1, q_rank), mla_kvn_g[l].reshape(1, kv_rank),
                 pad_gain(mla_q_g[l]), pad_gain(mla_k_g[l]),
                 jnp.tile(diff_q_g[l], 2).reshape(1, LANES), jnp.tile(diff_k_g[l], 2).reshape(1, LANES)]
        prep_w = (w_uq_p, w_k_p, w_v_p)
        w_ffn_in_p = w_ffn_in[l].astype(BF16)
        w_ffn_out_p = w_ffn_out[l].astype(BF16)
        lru_params = (conv_w[l], conv_b[l],
                      lru_wa[l].reshape(2 * RNN_BLOCKS, blk_w, blk_w).astype(BF16),
                      lru_wi[l].reshape(2 * RNN_BLOCKS, blk_w, blk_w).astype(BF16),
                      lru_ba[l], lru_bi[l], lru_lambda[l])
        merge_w = (diff_subln_g[l].reshape(1, DIFF_V), w_br_a[l].astype(BF16), w_br_b[l].astype(BF16),
                   w_br_c[l].astype(BF16), w_out[l].astype(BF16))

        mod6 = mod[l].reshape(mod.shape[1], 6, d)
        lat_mod = [mod6[:batch, i].reshape(batch, 1, d) for i in range(6)]
        ctx_mod = [mod6[batch:batch + 1, i].reshape(1, 1, d) for i in range(6)]

        yc = _inproj_call(xc, norm1_g[l], ctx_mod[1], ctx_mod[0], w_in_p)
        yc3 = yc.reshape(batch, t_ctx, yc.shape[1])
        yfc, ybc, h_ctx = _lru_call(yc3, jnp.zeros((2, batch, d), F32), *lru_params)
        qmc, kmc, vmc, qdc, kdc = _prep_call(yc, (), gains, prep_w, (ones_blk,), t_ctx)

        yl = _inproj_call(xl, norm1_g[l], lat_mod[1], lat_mod[0], w_in_p)
        yl3 = yl.reshape(batch, t_lat, yl.shape[1])
        yfl, ybl, _ = _lru_call(yl3, h_ctx, *lru_params)
        qml, kml, vml, qdl, kdl = _prep_call(yl, tabs, gains, prep_w, (ones_blk,) + perms, t_lat)

        mla_hps, diff_hps = 8, 4
        mla_kw = dict(batch=batch, heads_per_step=mla_hps, qw=LANES, vw=MLA_V, n_heads=MLA_HEADS, n_pbuf=2)
        diff_kw = dict(batch=batch, heads_per_step=diff_hps, qw=LANES, vw=DIFF_V, n_heads=DIFF_HEADS, n_pbuf=2)
        diff_kernel = functools.partial(_diff_attn_kernel, lam_init=lam_init)
        dv_col0 = COL_DV * (DIFF_HEADS // diff_hps)
        tq = 512
        o_b = _attn_call(_mla_attn_kernel, (), qml, [(kml, vml, t_lat), (kmc, vmc, t_ctx)],
                         t_q=t_lat, tq=tq, vcol0s=(0, 0), name="mla_attention", **mla_kw)
        o_c = _attn_call(diff_kernel, (diff_lambda[l],), qdl, [(kdl, yl, t_lat), (kdc, yc, t_ctx)],
                         t_q=t_lat, tq=tq, vcol0s=(dv_col0, dv_col0), name="diff_attention", **diff_kw)
        xl_new = _merge_call(xl, yfl.reshape(-1, d), ybl.reshape(-1, d), yl, o_b, o_c, lat_mod[2],
                             *merge_w, lam_init)
        xl = _ffn_call(xl_new, norm2_g[l], lat_mod[4], lat_mod[3], lat_mod[5], w_ffn_in_p, w_ffn_out_p, th)

        if need_ctx:
            o_bc = _attn_call(_mla_attn_kernel, (), qmc, [(kmc, vmc, t_ctx)],
                              t_q=t_ctx, tq=t_ctx, vcol0s=(0,), name="mla_attention_ctx", **mla_kw)
            o_cc = _attn_call(diff_kernel, (diff_lambda[l],), qdc, [(kdc, yc, t_ctx)],
                              t_q=t_ctx, tq=t_ctx, vcol0s=(dv_col0,), name="diff_attention_ctx", **diff_kw)
            xc_new = _merge_call(xc, yfc.reshape(-1, d), ybc.reshape(-1, d), yc, o_bc, o_cc, ctx_mod[2],
                                 *merge_w, lam_init)
            xc = _ffn_call(xc_new, norm2_g[l], ctx_mod[4], ctx_mod[3], ctx_mod[5], w_ffn_in_p, w_ffn_out_p, th)

    return xl.reshape(batch, t_lat, d)
```

```python
import functools
import math

import numpy as np
import jax
import jax.numpy as jnp
from jax import lax
from jax.experimental import pallas as pl
from jax.experimental.pallas import tpu as pltpu

F32 = jnp.float32
BF16 = jnp.bfloat16

EPS = 1e-6
GRID_W = 64
ROPE_BASE = 10000.0
LRU_C = 8.0
CONV_W = 4
RNN_BLOCKS = 8
MLA_HEADS = 16
MLA_NOPE = 64
MLA_ROPE = 32
MLA_V = 64
MLA_SCALE = (MLA_NOPE + MLA_ROPE) ** -0.5
DIFF_HEADS = 8
DIFF_HD = 64
DIFF_V = 2 * DIFF_HD
DIFF_SCALE = DIFF_HD ** -0.5
N_BRANCH = 3
LOG2E = math.log2(math.e)

LANES = 128
SUBLANES = 8
BF16_ROWS = 16
VMEM_LIMIT = 56 * 1024 * 1024

COL_MLA, COL_RX, COL_RG, COL_DQ, COL_DK, COL_DV, COL_MG = 0, 1, 2, 3, 4, 5, 6


def _cparams(n_axes, flags=None):
    return pltpu.CompilerParams(dimension_semantics=("arbitrary",) * n_axes,
                                vmem_limit_bytes=VMEM_LIMIT, flags=flags)


def _full(a, n_axes):
    return pl.BlockSpec(a.shape, lambda *_: (0,) * a.ndim)


def _sigmoid(x):
    return 0.5 * jnp.tanh(0.5 * x) + 0.5


def _silu(x):
    return x * _sigmoid(x)


def _gelu_tanh(x):
    return 0.5 * x * (1.0 + jnp.tanh(math.sqrt(2.0 / math.pi) * (x + 0.044715 * (x * x * x))))


def _rms(x, denom):
    ms = jnp.sum(x * x, axis=-1, keepdims=True) * (1.0 / denom)
    return x * lax.rsqrt(ms + EPS)


def _mod_kernel(c_ref, w_ref, b_ref, o_ref):
    s = _silu(c_ref[...]).astype(BF16)
    o_ref[...] = jnp.dot(s, w_ref[...].astype(BF16), preferred_element_type=F32) + b_ref[...]


def _mod_call(cc, w_mod, b_mod):
    depth, d, n = w_mod.shape
    tn = 1024
    return pl.pallas_call(
        _mod_kernel,
        grid=(depth, n // tn),
        in_specs=[pl.BlockSpec((cc.shape[0], d), lambda l, j: (0, 0)),
                  pl.BlockSpec((None, d, tn), lambda l, j: (l, 0, j)),
                  pl.BlockSpec((None, 1, tn), lambda l, j: (l, 0, j))],
        out_specs=pl.BlockSpec((None, cc.shape[0], tn), lambda l, j: (l, 0, j)),
        out_shape=jax.ShapeDtypeStruct((depth, cc.shape[0], n), F32),
        compiler_params=_cparams(2),
        name="adaln_mod",
    )(cc, w_mod, b_mod.reshape(depth, 1, n))


def _inproj_kernel(x_ref, g_ref, sc_ref, sh_ref, w_ref, o_ref, h_scr):
    @pl.when(pl.program_id(1) == 0)
    def _():
        xn = _rms(x_ref[...], x_ref.shape[-1]) * g_ref[...]
        h_scr[...] = (xn * (1.0 + sc_ref[...]) + sh_ref[...]).astype(BF16)

    o_ref[...] = jnp.dot(h_scr[...], w_ref[...], preferred_element_type=F32).astype(o_ref.dtype)


def _inproj_call(xs, norm_g, sc, sh, w_in_p):
    rows, d = xs.shape
    n = w_in_p.shape[1]
    tm, tn = min(1024, rows // sc.shape[0]), 1024
    per_seq = rows // tm // sc.shape[0]
    seq = lambda i, j: (i // per_seq, 0, 0)
    return pl.pallas_call(
        _inproj_kernel,
        grid=(rows // tm, n // tn),
        in_specs=[pl.BlockSpec((tm, d), lambda i, j: (i, 0)),
                  pl.BlockSpec((1, d), lambda i, j: (0, 0)),
                  pl.BlockSpec((None, 1, d), seq),
                  pl.BlockSpec((None, 1, d), seq),
                  pl.BlockSpec((d, tn), lambda i, j: (0, j))],
        out_specs=pl.BlockSpec((tm, tn), lambda i, j: (i, j)),
        out_shape=jax.ShapeDtypeStruct((rows, n), BF16),
        scratch_shapes=[pltpu.VMEM((tm, d), BF16)],
        compiler_params=_cparams(2),
        name="norm_inproj",
    )(xs, norm_g.reshape(1, d), sc, sh, w_in_p)


def _prep_kernel(*refs, use_rope):
    if use_rope:
        (ymla_ref, ydq_ref, ydk_ref, ydv_ref, cm_ref, sm_ref, cd_ref, sd_ref,
         qn_g_ref, kvn_g_ref, q_g_ref, k_g_ref, dq_g_ref, dk_g_ref,
         wuq_ref, wk_ref, wv_ref, ones_ref, pm_ref, pd_ref,
         qm_ref, km_ref, vm_ref, qd_ref, kd_ref, vd_ref) = refs
        cm, sm, cd, sd = cm_ref[...], sm_ref[...], cd_ref[...], sd_ref[...]
        pm, pd = pm_ref[...], pd_ref[...]
    else:
        (ymla_ref, ydq_ref, ydk_ref, ydv_ref,
         qn_g_ref, kvn_g_ref, q_g_ref, k_g_ref, dq_g_ref, dk_g_ref,
         wuq_ref, wk_ref, wv_ref, ones_ref,
         qm_ref, km_ref, vm_ref, qd_ref, kd_ref, vd_ref) = refs
        cm = sm = cd = sd = pm = pd = None
    q_rank = qn_g_ref.shape[-1]
    kv_rank = kvn_g_ref.shape[-1]
    ymla = ymla_ref[...].astype(F32)
    cq = ymla[:, :q_rank]
    ckv = ymla[:, q_rank:q_rank + kv_rank]
    kr = ymla[:, q_rank + kv_rank:q_rank + kv_rank + LANES]
    head_dim = MLA_NOPE + MLA_ROPE

    def rope(u, cos, sin, perm):
        if not use_rope:
            return u
        return u * cos + jnp.dot(u.astype(BF16), perm, preferred_element_type=F32) * sin

    cqn = (_rms(cq, q_rank) * qn_g_ref[...]).astype(BF16)
    q = jnp.dot(cqn, wuq_ref[...], preferred_element_type=F32)
    q_gain = q_g_ref[...] * (MLA_SCALE * LOG2E)
    for h in range(MLA_HEADS):
        u = _rms(q[:, h * LANES:(h + 1) * LANES], head_dim) * q_gain
        qm_ref[:, h * LANES:(h + 1) * LANES] = rope(u, cm, sm, pm).astype(BF16)

    ckvn = (_rms(ckv, kv_rank) * kvn_g_ref[...]).astype(BF16)
    vm = jnp.dot(ckvn, wv_ref[...], preferred_element_type=F32)
    sum_lane = (lax.broadcasted_iota(jnp.int32, vm.shape, 1) & (LANES - 1)) == MLA_V
    vm_ref[...] = jnp.where(sum_lane, 1.0, vm).astype(BF16)
    kn = jnp.dot(ckvn, wk_ref[...], preferred_element_type=F32)
    k_gain = k_g_ref[...]
    for h in range(MLA_HEADS):
        u = _rms(kn[:, h * LANES:(h + 1) * LANES] + kr, head_dim) * k_gain
        km_ref[:, h * LANES:(h + 1) * LANES] = rope(u, cm, sm, pm).astype(BF16)

    ones_blk = ones_ref[...]

    def diff_norm_rope(y_ref, gain, o_ref):
        for h in range(DIFF_HEADS):
            x = y_ref[:, h * LANES:(h + 1) * LANES].astype(F32)
            ms = jnp.dot((x * x).astype(BF16), ones_blk, preferred_element_type=F32) * (1.0 / DIFF_HD)
            u = x * lax.rsqrt(ms + EPS) * gain
            o_ref[:, h * LANES:(h + 1) * LANES] = rope(u, cd, sd, pd).astype(BF16)

    diff_norm_rope(ydq_ref, dq_g_ref[...] * (DIFF_SCALE * LOG2E), qd_ref)
    diff_norm_rope(ydk_ref, dk_g_ref[...], kd_ref)

    tm = ydv_ref.shape[0]
    one_hot = (lax.broadcasted_iota(jnp.int32, (tm, LANES), 1) == 0).astype(BF16)
    for h in range(DIFF_HEADS):
        vd_ref[:, 2 * h * LANES:(2 * h + 1) * LANES] = ydv_ref[:, h * LANES:(h + 1) * LANES]
        vd_ref[:, (2 * h + 1) * LANES:(2 * h + 2) * LANES] = one_hot


def _prep_call(y, tabs, gains, weights, consts, t_seq):
    rows = y.shape[0]
    d = 1024
    tm = 256
    use_rope = len(tabs) > 0
    per_seq = t_seq // tm
    row_blk = lambda w: pl.BlockSpec((tm, w), lambda i: (i, 0))
    col_blk = lambda c: pl.BlockSpec((tm, d), lambda i, c=c: (i, c))
    tab_blk = pl.BlockSpec((tm, LANES), lambda i: (i % per_seq, 0))
    wq = MLA_HEADS * LANES
    out_shapes = [jax.ShapeDtypeStruct((rows, wq), BF16), jax.ShapeDtypeStruct((rows, wq), BF16),
                  jax.ShapeDtypeStruct((rows, wq), BF16),
                  jax.ShapeDtypeStruct((rows, d), BF16), jax.ShapeDtypeStruct((rows, d), BF16),
                  jax.ShapeDtypeStruct((rows, 2 * d), BF16)]
    small = list(gains) + list(weights) + list(consts)
    return pl.pallas_call(
        functools.partial(_prep_kernel, use_rope=use_rope),
        grid=(rows // tm,),
        in_specs=[col_blk(COL_MLA), col_blk(COL_DQ), col_blk(COL_DK), col_blk(COL_DV)]
                 + [tab_blk] * len(tabs) + [_full(a, 1) for a in small],
        out_specs=[row_blk(s.shape[1]) for s in out_shapes],
        out_shape=out_shapes,
        compiler_params=_cparams(1),
        name="attn_prep_rope" if use_rope else "attn_prep",
    )(y, y, y, y, *tabs, *small)


def _lru_kernel(pf_ref, cf_ref, nf_ref, pb_ref, cb_ref, nb_ref, h0_ref,
                cw_ref, cbias_ref, wa_ref, wi_ref, ba_ref, bi_ref, lam_ref,
                yf_ref, yb_ref, hfin_ref, h_scr, win_scr, a_scr, b_scr, y_scr):
    k = pl.program_id(0)
    nk = pl.num_programs(0)
    batch, tt, w = cf_ref.shape
    halo = pf_ref.shape[1]
    tm = tt * batch
    blk_w = w // RNN_BLOCKS

    @pl.when(k == 0)
    def _():
        h_scr[...] = h0_ref[...]

    def interleave(src_ref, row0, n_steps, keep):
        for b in range(batch):
            v = src_ref[b].astype(F32)
            if keep is not None:
                v = jnp.where(keep, v, 0.0)
            for n in range(RNN_BLOCKS):
                win_scr[n, pl.ds(row0 + b, n_steps, stride=batch), :] = v[:, n * blk_w:(n + 1) * blk_w]

    def direction(r, prev_ref, cur_ref, next_ref, has_prev, has_next, y_ref):
        interleave(prev_ref, 0, halo, has_prev)
        interleave(cur_ref, halo * batch, tt, None)
        interleave(next_ref, (halo + tt) * batch, halo, has_next)
        base = halo * batch
        for n in range(RNN_BLOCKS):
            lanes = slice(n * blk_w, (n + 1) * blk_w)
            xc = cbias_ref[:, lanes]
            for j in range(CONV_W):
                off = base + (j - CONV_W // 2) * batch
                xc = xc + win_scr[n, off:off + tm, :] * cw_ref[j:j + 1, lanes]
            xb = xc.astype(BF16)
            gr = jnp.dot(xb, wa_ref[r * RNN_BLOCKS + n], preferred_element_type=F32) + ba_ref[r][:, lanes]
            gi = jnp.dot(xb, wi_ref[r * RNN_BLOCKS + n], preferred_element_type=F32) + bi_ref[r][:, lanes]
            lam = lam_ref[r][:, lanes]
            softplus_neg = jnp.maximum(-lam, 0.0) + jnp.log(1.0 + jnp.exp(-jnp.abs(lam)))
            a = jnp.exp2(_sigmoid(gr) * (softplus_neg * (-LRU_C * LOG2E)))
            a_scr[n] = a
            b_scr[n] = jnp.sqrt(1.0 - a * a) * _sigmoid(gi) * xc
        h = [h_scr[r, :, n * blk_w:(n + 1) * blk_w] for n in range(RNN_BLOCKS)]
        order = range(tt) if r == 0 else range(tt - 1, -1, -1)
        for j in order:
            rows = pl.ds(j * batch, batch)
            for n in range(RNN_BLOCKS):
                h[n] = a_scr[n, rows, :] * h[n] + b_scr[n, rows, :]
                y_scr[n, rows, :] = h[n]
        for n in range(RNN_BLOCKS):
            h_scr[r, :, n * blk_w:(n + 1) * blk_w] = h[n]
        for b in range(batch):
            for n in range(RNN_BLOCKS):
                y_ref[b, :, n * blk_w:(n + 1) * blk_w] = (
                    y_scr[n, pl.ds(b, tt, stride=batch), :].astype(y_ref.dtype))

    direction(0, pf_ref, cf_ref, nf_ref, k > 0, k < nk - 1, yf_ref)
    direction(1, pb_ref, cb_ref, nb_ref, k < nk - 1, k > 0, yb_ref)

    @pl.when(k == nk - 1)
    def _():
        hfin_ref[...] = h_scr[...]


def _lru_call(y3, h0, conv_w, conv_b, wa, wi, ba, bi, lam):
    batch, t_seq, _ = y3.shape
    w = 1024
    tt = 64
    halo = BF16_ROWS
    per = tt // halo
    nk = t_seq // tt
    last = t_seq // halo - 1

    def specs(chunk):
        return [pl.BlockSpec((batch, halo, w), lambda s: (0, jnp.maximum(chunk(s) * per - 1, 0), COL_RX)),
                pl.BlockSpec((batch, tt, w), lambda s: (0, chunk(s), COL_RX)),
                pl.BlockSpec((batch, halo, w), lambda s: (0, jnp.minimum((chunk(s) + 1) * per, last), COL_RX))]

    fwd = lambda s: s
    bwd = lambda s: nk - 1 - s
    params = [conv_w, conv_b.reshape(1, w), wa, wi, ba.reshape(2, 1, w), bi.reshape(2, 1, w),
              lam.reshape(2, 1, w)]
    tm = tt * batch
    return pl.pallas_call(
        _lru_kernel,
        grid=(nk,),
        in_specs=specs(fwd) + specs(bwd) + [_full(h0, 1)] + [_full(a, 1) for a in params],
        out_specs=[pl.BlockSpec((batch, tt, w), lambda s: (0, fwd(s), 0)),
                   pl.BlockSpec((batch, tt, w), lambda s: (0, bwd(s), 0)),
                   pl.BlockSpec((2, batch, w), lambda s: (0, 0, 0))],
        out_shape=[jax.ShapeDtypeStruct((batch, t_seq, w), BF16)] * 2
                  + [jax.ShapeDtypeStruct((2, batch, w), F32)],
        scratch_shapes=[pltpu.VMEM((2, batch, w), F32),
                        pltpu.VMEM((RNN_BLOCKS, (tt + 2 * halo) * batch, w // RNN_BLOCKS), F32)]
                       + [pltpu.VMEM((RNN_BLOCKS, tm, w // RNN_BLOCKS), F32)] * 3,
        compiler_params=_cparams(1),
        name="conv_rglru_scan",
    )(y3, y3, y3, y3, y3, y3, h0, *params)


KEY_CHUNK = 512
ROW_STRIP = 64


def _key_chunks(kv_pieces):
    chunks = []
    col = 0
    for k_ref, v_ref in kv_pieces:
        n = k_ref.shape[0]
        kc = min(KEY_CHUNK, n)
        for r0 in range(0, n, kc):
            chunks.append((k_ref, v_ref, r0, col, kc))
            col += kc
    return chunks


def _lane_tiles(c0, kc):
    return [slice(c0 + i * LANES, c0 + (i + 1) * LANES) for i in range(kc // LANES)]


def _row_strips(tq):
    return [slice(r * ROW_STRIP, (r + 1) * ROW_STRIP) for r in range(tq // ROW_STRIP)]


def _pv_chunk(p_scr, chunk, vcols):
    _, v_ref, r0, c0, kc = chunk
    return jnp.dot(p_scr[:, c0:c0 + kc], v_ref[r0:r0 + kc, vcols], preferred_element_type=F32)


def _score_exp(q, chunks, cols, s_scr, p_scr, want_sum):
    tq = q.shape[0]
    for k_ref, _, r0, c0, kc in chunks:
        s_scr[:, c0:c0 + kc] = lax.dot_general(q, k_ref[r0:r0 + kc, cols], (((1,), (1,)), ((), ())),
                                               preferred_element_type=F32)
    tiles = [lanes for chunk in chunks for lanes in _lane_tiles(chunk[3], chunk[4])]
    sums = []
    for rows in _row_strips(tq):
        m = functools.reduce(jnp.maximum, [s_scr[rows, lanes] for lanes in tiles])
        m = jnp.broadcast_to(jnp.max(m, axis=-1, keepdims=True), (ROW_STRIP, LANES))
        lpart = None
        for lanes in tiles:
            e = jnp.exp2(s_scr[rows, lanes] - m)
            if want_sum:
                lpart = e if lpart is None else lpart + e
            p_scr[rows, lanes] = e.astype(BF16)
        if want_sum:
            sums.append(jnp.sum(lpart, axis=-1, keepdims=True))
    return jnp.concatenate(sums, axis=0) if want_sum else None


def _softmax_pv(q, kv_pieces, cols, vcols, s_scr, p_scr, sum_lane=None):
    chunks = _key_chunks(kv_pieces)
    l = _score_exp(q, chunks, cols, s_scr, p_scr, want_sum=sum_lane is None)
    acc = functools.reduce(jnp.add, [_pv_chunk(p_scr, chunk, vcols) for chunk in chunks])
    if sum_lane is not None:
        l = acc[:, sum_lane:sum_lane + 1]
    return acc * (1.0 / l)


class _UnitRunner:
    def __init__(self, n_units, tq, kv, s_scr, p_scr, sum_lane=None):
        self.n_units, self.tq, self.kv, self.s_scr, self.p_scr = n_units, tq, kv, s_scr, p_scr
        self.sum_lane = sum_lane
        self.done = 0
        self.pieces = 0

    def run(self, q_of_rows, cols, vcols):
        edge = self.done in (0, self.n_units - 1)
        self.done += 1
        half = self.tq // 2
        if edge and half % ROW_STRIP == 0:
            row_sets = [pl.ds(0, half), pl.ds(half, half)]
        else:
            row_sets = [pl.ds(0, self.tq)]
        outs = []
        for rows in row_sets:
            buf = self.pieces % 2
            self.pieces += 1
            outs.append(_softmax_pv(q_of_rows(rows), self.kv, cols, vcols,
                                    self.s_scr.at[buf, rows], self.p_scr.at[buf, rows], self.sum_lane))
        return outs[0] if len(outs) == 1 else jnp.concatenate(outs, axis=0)


def _mla_attn_kernel(q_ref, *refs):
    o_ref, s_scr, p_scr = refs[-3:]
    kv = [(refs[i], refs[i + 1]) for i in range(0, len(refs) - 3, 2)]
    tq = q_ref.shape[0]
    n_heads = q_ref.shape[1] // LANES
    lane = lax.broadcasted_iota(jnp.int32, (tq, LANES), 1)
    units = _UnitRunner(n_heads, tq, kv, s_scr, p_scr, sum_lane=MLA_V)
    for p in range(n_heads // 2):
        outs = []
        for hh in (2 * p, 2 * p + 1):
            cols = slice(hh * LANES, (hh + 1) * LANES)
            outs.append(units.run(lambda rows, cols=cols: q_ref[rows, cols], cols, cols))
        pair = jnp.where(lane < MLA_V, outs[0], pltpu.roll(outs[1], MLA_V, axis=1))
        o_ref[:, p * LANES:(p + 1) * LANES] = pair.astype(o_ref.dtype)


def _diff_attn_kernel(dl_ref, q_ref, *refs, lam_init):
    o_ref, s_scr, p_scr = refs[-3:]
    kv = [(refs[i], refs[i + 1]) for i in range(0, len(refs) - 3, 2)]
    dl = dl_ref[...]
    lam = (jnp.exp(jnp.sum(dl[0:1] * dl[1:2], axis=-1, keepdims=True))
           - jnp.exp(jnp.sum(dl[2:3] * dl[3:4], axis=-1, keepdims=True)) + lam_init)
    n_heads = q_ref.shape[1] // LANES
    units = _UnitRunner(2 * n_heads, q_ref.shape[0], kv, s_scr, p_scr, sum_lane=DIFF_V)
    for h in range(n_heads):
        cols = slice(h * LANES, (h + 1) * LANES)
        vcols = slice(2 * h * LANES, (2 * h + 2) * LANES)

        def map_q(rows, first, cols=cols):
            q = q_ref[rows, cols].astype(F32)
            in_first = lax.broadcasted_iota(jnp.int32, q.shape, 1) < DIFF_HD
            return (jnp.where(in_first, q, 0.0) if first else jnp.where(in_first, 0.0, q)).astype(BF16)

        o0 = units.run(functools.partial(map_q, first=True), cols, vcols)[:, :DIFF_V]
        o1 = units.run(functools.partial(map_q, first=False), cols, vcols)[:, :DIFF_V]
        o_ref[:, cols] = (o0 - lam * o1).astype(o_ref.dtype)


def _attn_call(kernel, extra, q, kv_arrays, *, batch, t_q, tq, heads_per_step, qw, vw, ow, n_heads, vcol0s,
               n_pbuf, name):
    groups = n_heads // heads_per_step
    nq = t_q // tq
    n_keys = sum(t_kv for _, _, t_kv in kv_arrays)
    kv_specs, kv_args = [], []
    for (k, v, t_kv), vcol0 in zip(kv_arrays, vcol0s):
        kv_specs += [pl.BlockSpec((t_kv, heads_per_step * qw), lambda b, g, i: (b, g)),
                     pl.BlockSpec((t_kv, heads_per_step * vw), lambda b, g, i, c=vcol0: (b, c + g))]
        kv_args += [k, v]
    return pl.pallas_call(
        kernel,
        grid=(batch, groups, nq),
        in_specs=[_full(a, 3) for a in extra]
                 + [pl.BlockSpec((tq, heads_per_step * qw), lambda b, g, i: (b * nq + i, g))] + kv_specs,
        out_specs=pl.BlockSpec((tq, heads_per_step * ow), lambda b, g, i: (b * nq + i, g)),
        out_shape=jax.ShapeDtypeStruct((batch * t_q, n_heads * ow), BF16),
        scratch_shapes=[pltpu.VMEM((2, tq, n_keys), F32), pltpu.VMEM((n_pbuf, tq, n_keys), BF16)],
        compiler_params=_cparams(3),
        name=name,
    )(*extra, q, *kv_args)


def _merge_kernel(x_ref, yf_ref, yb_ref, rg_ref, ob_ref, oc_ref, mg_ref, g1_ref, sub_g_ref,
                  wa_ref, wb_ref, wc_ref, wo_ref, o_ref, *, lam_init):
    d = x_ref.shape[-1]
    ya = yf_ref[...].astype(F32) + yb_ref[...].astype(F32)
    za = (ya * _gelu_tanh(rg_ref[...].astype(F32))).astype(BF16)
    br_a = jnp.dot(za, wa_ref[...], preferred_element_type=F32)
    br_b = jnp.dot(ob_ref[...], wb_ref[...], preferred_element_type=F32)
    oc = oc_ref[...].astype(F32)
    sub_gain = sub_g_ref[...] * (1.0 - lam_init)
    oc_n = jnp.concatenate(
        [_rms(oc[:, h * DIFF_V:(h + 1) * DIFF_V], DIFF_V) * sub_gain for h in range(DIFF_HEADS)], axis=1)
    br_c = jnp.dot(oc_n.astype(BF16), wc_ref[...], preferred_element_type=F32)
    mg = mg_ref[...].astype(F32)
    mix = (_sigmoid(mg[:, :d]) * br_a + _sigmoid(mg[:, d:2 * d]) * br_b
           + _sigmoid(mg[:, 2 * d:]) * br_c)
    m = jnp.dot(mix.astype(BF16), wo_ref[...], preferred_element_type=F32)
    o_ref[...] = x_ref[...] + g1_ref[...] * m


def _merge_call(xs, yf, yb, y, ob, oc, g1, sub_g, wa, wb, wc, wo, lam_init):
    rows, d = xs.shape
    tm = 256
    per_seq = rows // tm // g1.shape[0]
    blk = lambda: pl.BlockSpec((tm, d), lambda i: (i, 0))
    return pl.pallas_call(
        functools.partial(_merge_kernel, lam_init=lam_init),
        grid=(rows // tm,),
        in_specs=[blk(), blk(), blk(),
                  pl.BlockSpec((tm, d), lambda i: (i, COL_RG)),
                  blk(), blk(),
                  pl.BlockSpec((tm, N_BRANCH * d), lambda i: (i, COL_MG // N_BRANCH)),
                  pl.BlockSpec((None, 1, d), lambda i: (i // per_seq, 0, 0)),
                  _full(sub_g, 1), _full(wa, 1), _full(wb, 1), _full(wc, 1), _full(wo, 1)],
        out_specs=blk(),
        out_shape=jax.ShapeDtypeStruct((rows, d), F32),
        compiler_params=_cparams(1),
        name="branch_merge",
    )(xs, yf, yb, y, ob, oc, y, g1, sub_g, wa, wb, wc, wo)


def _ffn_kernel(x_ref, g_ref, sc_ref, sh_ref, g2_ref, wg_ref, wu_ref, wout_ref, o_ref, h_scr, acc_scr):
    j = pl.program_id(1)

    @pl.when(j == 0)
    def _():
        xn = _rms(x_ref[...], x_ref.shape[-1]) * g_ref[...]
        h_scr[...] = (xn * (1.0 + sc_ref[...]) + sh_ref[...]).astype(BF16)
        acc_scr[...] = jnp.zeros_like(acc_scr)

    h = h_scr[...]
    gate = jnp.dot(h, wg_ref[...], preferred_element_type=F32)
    up = jnp.dot(h, wu_ref[...], preferred_element_type=F32)
    act = (_silu(gate) * up).astype(BF16)
    acc_scr[...] += jnp.dot(act, wout_ref[...], preferred_element_type=F32)

    @pl.when(j == pl.num_programs(1) - 1)
    def _():
        o_ref[...] = x_ref[...] + g2_ref[...] * acc_scr[...]


def _ffn_call(xs, norm_g, sc, sh, g2, w_in, w_out, th):
    rows, d = xs.shape
    hidden = w_out.shape[0]
    nj = hidden // th
    tm = 512
    per_seq = rows // tm // sc.shape[0]
    seq = lambda i, j: (i // per_seq, 0, 0)
    return pl.pallas_call(
        _ffn_kernel,
        grid=(rows // tm, nj),
        in_specs=[pl.BlockSpec((tm, d), lambda i, j: (i, 0)),
                  pl.BlockSpec((1, d), lambda i, j: (0, 0)),
                  pl.BlockSpec((None, 1, d), seq),
                  pl.BlockSpec((None, 1, d), seq),
                  pl.BlockSpec((None, 1, d), seq),
                  pl.BlockSpec((d, th), lambda i, j: (0, j)),
                  pl.BlockSpec((d, th), lambda i, j: (0, nj + j)),
                  pl.BlockSpec((th, d), lambda i, j: (j, 0))],
        out_specs=pl.BlockSpec((tm, d), lambda i, j: (i, 0)),
        out_shape=jax.ShapeDtypeStruct((rows, d), F32),
        scratch_shapes=[pltpu.VMEM((tm, d), BF16), pltpu.VMEM((tm, d), F32)],
        compiler_params=_cparams(2),
        name="norm_swiglu_ffn",
    )(xs, norm_g.reshape(1, d), sc, sh, g2, w_in, w_in, w_out)


def _rope_lane_tables(t_lat, rot_dim, lane_of_pair, copies):
    rows_n = t_lat // GRID_W
    row_ids = jnp.repeat(jnp.arange(rows_n, dtype=F32), GRID_W)
    col_ids = jnp.tile(jnp.arange(GRID_W, dtype=F32), rows_n)
    n = rot_dim // 4
    freqs = ROPE_BASE ** (-jnp.arange(n, dtype=F32) / n)
    ang = jnp.concatenate([row_ids[:, None] * freqs, col_ids[:, None] * freqs], axis=-1)
    cos, sin = jnp.cos(ang), jnp.sin(ang)
    npairs = rot_dim // 2
    sel_c = np.zeros((npairs, LANES), np.float32)
    sel_s = np.zeros((npairs, LANES), np.float32)
    base_c = np.ones((LANES,), np.float32)
    for off in copies:
        for i in range(npairs):
            lane = off + lane_of_pair(i)
            sel_c[i, lane] = sel_c[i, lane + 1] = 1.0
            sel_s[i, lane] = -1.0
            sel_s[i, lane + 1] = 1.0
            base_c[lane] = base_c[lane + 1] = 0.0
    pick_c = np.argmax(sel_c, axis=0)
    pick_s = np.argmax(np.abs(sel_s), axis=0)
    c_tab = jnp.where(jnp.asarray(base_c > 0)[None, :], 1.0, cos[:, pick_c])
    s_tab = sin[:, pick_s] * jnp.asarray(sel_s.sum(axis=0))[None, :]
    return c_tab, s_tab


def _pair_swap_matrix(rot_dim, lane_of_pair, copies):
    p = np.zeros((LANES, LANES), np.float32)
    for off in copies:
        for i in range(rot_dim // 2):
            lane = off + lane_of_pair(i)
            p[lane + 1, lane] = 1.0
            p[lane, lane + 1] = 1.0
    return jnp.asarray(p, BF16)


def kernel(x, c, ctx, c_ctx, w_mod, b_mod, norm1_g, norm2_g, w_in, conv_w, conv_b, lru_wa, lru_ba, lru_wi, lru_bi, lru_lambda, mla_qn_g, mla_w_uq, mla_kvn_g, mla_w_ukv, mla_q_g, mla_k_g, diff_q_g, diff_k_g, diff_lambda, diff_subln_g, w_br_a, w_br_b, w_br_c, w_out, w_ffn_in, w_ffn_out):
    batch, t_lat, d = x.shape
    t_ctx = ctx.shape[1]
    depth = w_mod.shape[0]
    assert batch == SUBLANES and d == 1024
    q_rank = mla_qn_g.shape[1]
    kv_rank = mla_kvn_g.shape[1]
    hidden = w_ffn_out.shape[1]
    th = hidden // 2
    head_dim = MLA_NOPE + MLA_ROPE

    xl = x.reshape(batch * t_lat, d)
    xc = ctx.reshape(batch * t_ctx, d)

    cc = jnp.concatenate([c, c_ctx[None, :], jnp.zeros((BF16_ROWS - batch - 1, d), F32)], axis=0)
    mod = _mod_call(cc, w_mod, b_mod)

    mla_pair_lane = lambda i: MLA_NOPE + 2 * i
    diff_pair_lane = lambda i: 2 * i
    tabs = (_rope_lane_tables(t_lat, MLA_ROPE, mla_pair_lane, (0,))
            + _rope_lane_tables(t_lat, DIFF_HD, diff_pair_lane, (0, DIFF_HD)))
    perms = (_pair_swap_matrix(MLA_ROPE, mla_pair_lane, (0,)),
             _pair_swap_matrix(DIFF_HD, diff_pair_lane, (0, DIFF_HD)))
    ones_np = np.zeros((LANES, LANES), np.float32)
    ones_np[:DIFF_HD, :DIFF_HD] = 1.0
    ones_np[DIFF_HD:, DIFF_HD:] = 1.0
    ones_blk = jnp.asarray(ones_np, BF16)

    offs = np.cumsum([0, d, d, q_rank, kv_rank, MLA_ROPE, d, d, d, N_BRANCH * d])
    o_rx, o_rg, o_cq, o_ckv, o_kr, o_dq, o_dk, o_dv, o_mg = offs[:9]
    blk_w = d // RNN_BLOCKS

    for l in range(depth):
        need_ctx = l < depth - 1
        lam_init = 0.8 - 0.6 * math.exp(-0.3 * l)
        wl = w_in[l].astype(BF16)
        mla_cols = jnp.concatenate(
            [wl[:, o_cq:o_cq + q_rank], wl[:, o_ckv:o_ckv + kv_rank],
             jnp.zeros((d, MLA_NOPE), BF16), wl[:, o_kr:o_kr + MLA_ROPE],
             jnp.zeros((d, d - q_rank - kv_rank - head_dim), BF16)], axis=1)
        w_in_p = jnp.concatenate(
            [mla_cols, wl[:, o_rx:o_rx + d], wl[:, o_rg:o_rg + d], wl[:, o_dq:o_dq + d],
             wl[:, o_dk:o_dk + d], wl[:, o_dv:o_dv + d], wl[:, o_mg:o_mg + N_BRANCH * d]],
            axis=1)
        w_uq_p = jnp.pad(mla_w_uq[l].reshape(q_rank, MLA_HEADS, head_dim),
                         ((0, 0), (0, 0), (0, LANES - head_dim))).reshape(q_rank, MLA_HEADS * LANES).astype(BF16)
        w_ukv = mla_w_ukv[l].reshape(kv_rank, MLA_HEADS, MLA_NOPE + MLA_V)
        w_k_p = jnp.pad(w_ukv[:, :, :MLA_NOPE],
                        ((0, 0), (0, 0), (0, LANES - MLA_NOPE))).reshape(kv_rank, MLA_HEADS * LANES).astype(BF16)
        w_v_p = jnp.pad(w_ukv[:, :, MLA_NOPE:],
                        ((0, 0), (0, 0), (0, LANES - MLA_V))).reshape(kv_rank, MLA_HEADS * LANES).astype(BF16)
        pad_gain = lambda g: jnp.pad(g, (0, LANES - head_dim)).reshape(1, LANES)
        gains = [mla_qn_g[l].reshape(1, q_rank), mla_kvn_g[l].reshape(1, kv_rank),
                 pad_gain(mla_q_g[l]), pad_gain(mla_k_g[l]),
                 jnp.tile(diff_q_g[l], 2).reshape(1, LANES), jnp.tile(diff_k_g[l], 2).reshape(1, LANES)]
        prep_w = (w_uq_p, w_k_p, w_v_p)
        w_ffn_in_p = w_ffn_in[l].astype(BF16)
        w_ffn_out_p = w_ffn_out[l].astype(BF16)
        lru_params = (conv_w[l], conv_b[l],
                      lru_wa[l].reshape(2 * RNN_BLOCKS, blk_w, blk_w).astype(BF16),
                      lru_wi[l].reshape(2 * RNN_BLOCKS, blk_w, blk_w).astype(BF16),
                      lru_ba[l], lru_bi[l], lru_lambda[l])
        merge_w = (diff_subln_g[l].reshape(1, DIFF_V), w_br_a[l].astype(BF16), w_br_b[l].astype(BF16),
                   w_br_c[l].astype(BF16), w_out[l].astype(BF16))

        mod6 = mod[l].reshape(mod.shape[1], 6, d)
        lat_mod = [mod6[:batch, i].reshape(batch, 1, d) for i in range(6)]
        ctx_mod = [mod6[batch:batch + 1, i].reshape(1, 1, d) for i in range(6)]

        yc = _inproj_call(xc, norm1_g[l], ctx_mod[1], ctx_mod[0], w_in_p)
        yc3 = yc.reshape(batch, t_ctx, yc.shape[1])
        yfc, ybc, h_ctx = _lru_call(yc3, jnp.zeros((2, batch, d), F32), *lru_params)
        qmc, kmc, vmc, qdc, kdc, vdc = _prep_call(yc, (), gains, prep_w, (ones_blk,), t_ctx)

        yl = _inproj_call(xl, norm1_g[l], lat_mod[1], lat_mod[0], w_in_p)
        yl3 = yl.reshape(batch, t_lat, yl.shape[1])
        yfl, ybl, _ = _lru_call(yl3, h_ctx, *lru_params)
        qml, kml, vml, qdl, kdl, vdl = _prep_call(yl, tabs, gains, prep_w, (ones_blk,) + perms, t_lat)

        mla_hps, diff_hps = 8, 4
        mla_kw = dict(batch=batch, heads_per_step=mla_hps, qw=LANES, vw=LANES, ow=MLA_V, n_heads=MLA_HEADS,
                      n_pbuf=2)
        diff_kw = dict(batch=batch, heads_per_step=diff_hps, qw=LANES, vw=2 * DIFF_V, ow=DIFF_V,
                       n_heads=DIFF_HEADS, n_pbuf=2)
        diff_kernel = functools.partial(_diff_attn_kernel, lam_init=lam_init)
        tq = 512
        o_b = _attn_call(_mla_attn_kernel, (), qml, [(kml, vml, t_lat), (kmc, vmc, t_ctx)],
                         t_q=t_lat, tq=tq, vcol0s=(0, 0), name="mla_attention", **mla_kw)
        o_c = _attn_call(diff_kernel, (diff_lambda[l],), qdl, [(kdl, vdl, t_lat), (kdc, vdc, t_ctx)],
                         t_q=t_lat, tq=tq, vcol0s=(0, 0), name="diff_attention", **diff_kw)
        xl_new = _merge_call(xl, yfl.reshape(-1, d), ybl.reshape(-1, d), yl, o_b, o_c, lat_mod[2],
                             *merge_w, lam_init)
        xl = _ffn_call(xl_new, norm2_g[l], lat_mod[4], lat_mod[3], lat_mod[5], w_ffn_in_p, w_ffn_out_p, th)

        if need_ctx:
            o_bc = _attn_call(_mla_attn_kernel, (), qmc, [(kmc, vmc, t_ctx)],
                              t_q=t_ctx, tq=t_ctx, vcol0s=(0,), name="mla_attention_ctx", **mla_kw)
            o_cc = _attn_call(diff_kernel, (diff_lambda[l],), qdc, [(kdc, vdc, t_ctx)],
                              t_q=t_ctx, tq=t_ctx, vcol0s=(0,), name="diff_attention_ctx", **diff_kw)
            xc_new = _merge_call(xc, yfc.reshape(-1, d), ybc.reshape(-1, d), yc, o_bc, o_cc, ctx_mod[2],
                                 *merge_w, lam_init)
            xc = _ffn_call(xc_new, norm2_g[l], ctx_mod[4], ctx_mod[3], ctx_mod[5], w_ffn_in_p, w_ffn_out_p, th)

    return xl.reshape(batch, t_lat, d)
```

```python
import functools
import math

import numpy as np
import jax
import jax.numpy as jnp
from jax import lax
from jax.experimental import pallas as pl
from jax.experimental.pallas import tpu as pltpu

F32 = jnp.float32
BF16 = jnp.bfloat16

EPS = 1e-6
GRID_W = 64
ROPE_BASE = 10000.0
LRU_C = 8.0
CONV_W = 4
RNN_BLOCKS = 8
MLA_HEADS = 16
MLA_NOPE = 64
MLA_ROPE = 32
MLA_V = 64
MLA_SCALE = (MLA_NOPE + MLA_ROPE) ** -0.5
DIFF_HEADS = 8
DIFF_HD = 64
DIFF_V = 2 * DIFF_HD
DIFF_SCALE = DIFF_HD ** -0.5
N_BRANCH = 3
LOG2E = math.log2(math.e)

LANES = 128
SUBLANES = 8
BF16_ROWS = 16
VMEM_LIMIT = 56 * 1024 * 1024

COL_MLA, COL_RX, COL_RG, COL_DQ, COL_DK, COL_DV, COL_MG = 0, 1, 2, 3, 4, 5, 6


def _cparams(n_axes, flags=None):
    return pltpu.CompilerParams(dimension_semantics=("arbitrary",) * n_axes,
                                vmem_limit_bytes=VMEM_LIMIT, flags=flags)


def _full(a, n_axes):
    return pl.BlockSpec(a.shape, lambda *_: (0,) * a.ndim)


def _sigmoid(x):
    return 0.5 * jnp.tanh(0.5 * x) + 0.5


def _silu(x):
    return x * _sigmoid(x)


def _gelu_tanh(x):
    return 0.5 * x * (1.0 + jnp.tanh(math.sqrt(2.0 / math.pi) * (x + 0.044715 * (x * x * x))))


def _rms(x, denom):
    ms = jnp.sum(x * x, axis=-1, keepdims=True) * (1.0 / denom)
    return x * lax.rsqrt(ms + EPS)


def _mod_kernel(c_ref, w_ref, b_ref, o_ref):
    s = _silu(c_ref[...]).astype(BF16)
    o_ref[...] = jnp.dot(s, w_ref[...].astype(BF16), preferred_element_type=F32) + b_ref[...]


def _mod_call(cc, w_mod, b_mod):
    depth, d, n = w_mod.shape
    tn = 1024
    return pl.pallas_call(
        _mod_kernel,
        grid=(depth, n // tn),
        in_specs=[pl.BlockSpec((cc.shape[0], d), lambda l, j: (0, 0)),
                  pl.BlockSpec((None, d, tn), lambda l, j: (l, 0, j)),
                  pl.BlockSpec((None, 1, tn), lambda l, j: (l, 0, j))],
        out_specs=pl.BlockSpec((None, cc.shape[0], tn), lambda l, j: (l, 0, j)),
        out_shape=jax.ShapeDtypeStruct((depth, cc.shape[0], n), F32),
        compiler_params=_cparams(2),
        name="adaln_mod",
    )(cc, w_mod, b_mod.reshape(depth, 1, n))


def _inproj_kernel(x_ref, g_ref, sc_ref, sh_ref, w_ref, o_ref, h_scr):
    @pl.when(pl.program_id(1) == 0)
    def _():
        xn = _rms(x_ref[...], x_ref.shape[-1]) * g_ref[...]
        h_scr[...] = (xn * (1.0 + sc_ref[...]) + sh_ref[...]).astype(BF16)

    o_ref[...] = jnp.dot(h_scr[...], w_ref[...], preferred_element_type=F32).astype(o_ref.dtype)


def _inproj_call(xs, norm_g, sc, sh, w_in_p):
    rows, d = xs.shape
    n = w_in_p.shape[1]
    tm, tn = min(1024, rows // sc.shape[0]), 2304
    per_seq = rows // tm // sc.shape[0]
    seq = lambda i, j: (i // per_seq, 0, 0)
    return pl.pallas_call(
        _inproj_kernel,
        grid=(rows // tm, n // tn),
        in_specs=[pl.BlockSpec((tm, d), lambda i, j: (i, 0)),
                  pl.BlockSpec((1, d), lambda i, j: (0, 0)),
                  pl.BlockSpec((None, 1, d), seq),
                  pl.BlockSpec((None, 1, d), seq),
                  pl.BlockSpec((d, tn), lambda i, j: (0, j))],
        out_specs=pl.BlockSpec((tm, tn), lambda i, j: (i, j)),
        out_shape=jax.ShapeDtypeStruct((rows, n), BF16),
        scratch_shapes=[pltpu.VMEM((tm, d), BF16)],
        compiler_params=_cparams(2),
        name="norm_inproj",
    )(xs, norm_g.reshape(1, d), sc, sh, w_in_p)


def _prep_kernel(*refs, use_rope):
    if use_rope:
        (ymla_ref, ydq_ref, ydk_ref, ydv_ref, cm_ref, sm_ref, cd_ref, sd_ref,
         qn_g_ref, kvn_g_ref, q_g_ref, k_g_ref, dq_g_ref, dk_g_ref,
         wuq_ref, wk_ref, wv_ref, ones_ref, pm_ref, pd_ref,
         qm_ref, km_ref, vm_ref, qd_ref, kd_ref, vd_ref) = refs
        cm, sm, cd, sd = cm_ref[...], sm_ref[...], cd_ref[...], sd_ref[...]
        pm, pd = pm_ref[...], pd_ref[...]
    else:
        (ymla_ref, ydq_ref, ydk_ref, ydv_ref,
         qn_g_ref, kvn_g_ref, q_g_ref, k_g_ref, dq_g_ref, dk_g_ref,
         wuq_ref, wk_ref, wv_ref, ones_ref,
         qm_ref, km_ref, vm_ref, qd_ref, kd_ref, vd_ref) = refs
        cm = sm = cd = sd = pm = pd = None
    q_rank = qn_g_ref.shape[-1]
    kv_rank = kvn_g_ref.shape[-1]
    ymla = ymla_ref[...].astype(F32)
    cq = ymla[:, :q_rank]
    ckv = ymla[:, q_rank:q_rank + kv_rank]
    kr = ymla[:, q_rank + kv_rank:q_rank + kv_rank + LANES]
    head_dim = MLA_NOPE + MLA_ROPE

    def rope(u, cos, sin, perm):
        if not use_rope:
            return u
        return u * cos + jnp.dot(u.astype(BF16), perm, preferred_element_type=F32) * sin

    cqn = (_rms(cq, q_rank) * qn_g_ref[...]).astype(BF16)
    q = jnp.dot(cqn, wuq_ref[...], preferred_element_type=F32)
    q_gain = q_g_ref[...] * (MLA_SCALE * LOG2E)
    for h in range(MLA_HEADS):
        u = _rms(q[:, h * LANES:(h + 1) * LANES], head_dim) * q_gain
        qm_ref[:, h * LANES:(h + 1) * LANES] = rope(u, cm, sm, pm).astype(BF16)

    ckvn = (_rms(ckv, kv_rank) * kvn_g_ref[...]).astype(BF16)
    vm = jnp.dot(ckvn, wv_ref[...], preferred_element_type=F32)
    sum_lane = (lax.broadcasted_iota(jnp.int32, vm.shape, 1) & (LANES - 1)) == MLA_V
    vm_ref[...] = jnp.where(sum_lane, 1.0, vm).astype(BF16)
    kn = jnp.dot(ckvn, wk_ref[...], preferred_element_type=F32)
    k_gain = k_g_ref[...]
    for h in range(MLA_HEADS):
        u = _rms(kn[:, h * LANES:(h + 1) * LANES] + kr, head_dim) * k_gain
        km_ref[:, h * LANES:(h + 1) * LANES] = rope(u, cm, sm, pm).astype(BF16)

    ones_blk = ones_ref[...]

    def diff_norm_rope(y_ref, gain, o_ref):
        for h in range(DIFF_HEADS):
            x = y_ref[:, h * LANES:(h + 1) * LANES].astype(F32)
            ms = jnp.dot((x * x).astype(BF16), ones_blk, preferred_element_type=F32) * (1.0 / DIFF_HD)
            u = x * lax.rsqrt(ms + EPS) * gain
            o_ref[:, h * LANES:(h + 1) * LANES] = rope(u, cd, sd, pd).astype(BF16)

    diff_norm_rope(ydq_ref, dq_g_ref[...] * (DIFF_SCALE * LOG2E), qd_ref)
    diff_norm_rope(ydk_ref, dk_g_ref[...], kd_ref)

    tm = ydv_ref.shape[0]
    one_hot = (lax.broadcasted_iota(jnp.int32, (tm, LANES), 1) == 0).astype(BF16)
    for h in range(DIFF_HEADS):
        vd_ref[:, 2 * h * LANES:(2 * h + 1) * LANES] = ydv_ref[:, h * LANES:(h + 1) * LANES]
        vd_ref[:, (2 * h + 1) * LANES:(2 * h + 2) * LANES] = one_hot


def _prep_call(y, tabs, gains, weights, consts, t_seq):
    rows = y.shape[0]
    d = 1024
    tm = 256
    use_rope = len(tabs) > 0
    per_seq = t_seq // tm
    row_blk = lambda w: pl.BlockSpec((tm, w), lambda i: (i, 0))
    col_blk = lambda c: pl.BlockSpec((tm, d), lambda i, c=c: (i, c))
    tab_blk = pl.BlockSpec((tm, LANES), lambda i: (i % per_seq, 0))
    wq = MLA_HEADS * LANES
    out_shapes = [jax.ShapeDtypeStruct((rows, wq), BF16), jax.ShapeDtypeStruct((rows, wq), BF16),
                  jax.ShapeDtypeStruct((rows, wq), BF16),
                  jax.ShapeDtypeStruct((rows, d), BF16), jax.ShapeDtypeStruct((rows, d), BF16),
                  jax.ShapeDtypeStruct((rows, 2 * d), BF16)]
    small = list(gains) + list(weights) + list(consts)
    return pl.pallas_call(
        functools.partial(_prep_kernel, use_rope=use_rope),
        grid=(rows // tm,),
        in_specs=[col_blk(COL_MLA), col_blk(COL_DQ), col_blk(COL_DK), col_blk(COL_DV)]
                 + [tab_blk] * len(tabs) + [_full(a, 1) for a in small],
        out_specs=[row_blk(s.shape[1]) for s in out_shapes],
        out_shape=out_shapes,
        compiler_params=_cparams(1),
        name="attn_prep_rope" if use_rope else "attn_prep",
    )(y, y, y, y, *tabs, *small)


def _lru_kernel(pf_ref, cf_ref, nf_ref, pb_ref, cb_ref, nb_ref, h0_ref,
                cw_ref, cbias_ref, wa_ref, wi_ref, ba_ref, bi_ref, lam_ref,
                yf_ref, yb_ref, hfin_ref, h_scr, win_scr, a_scr, b_scr, y_scr):
    k = pl.program_id(0)
    nk = pl.num_programs(0)
    batch, tt, w = cf_ref.shape
    halo = pf_ref.shape[1]
    tm = tt * batch
    blk_w = w // RNN_BLOCKS

    @pl.when(k == 0)
    def _():
        h_scr[...] = h0_ref[...]

    def interleave(src_ref, row0, n_steps, keep):
        for b in range(batch):
            v = src_ref[b].astype(F32)
            if keep is not None:
                v = jnp.where(keep, v, 0.0)
            for n in range(RNN_BLOCKS):
                win_scr[n, pl.ds(row0 + b, n_steps, stride=batch), :] = v[:, n * blk_w:(n + 1) * blk_w]

    def direction(r, prev_ref, cur_ref, next_ref, has_prev, has_next, y_ref):
        interleave(prev_ref, 0, halo, has_prev)
        interleave(cur_ref, halo * batch, tt, None)
        interleave(next_ref, (halo + tt) * batch, halo, has_next)
        base = halo * batch
        for n in range(RNN_BLOCKS):
            lanes = slice(n * blk_w, (n + 1) * blk_w)
            xc = cbias_ref[:, lanes]
            for j in range(CONV_W):
                off = base + (j - CONV_W // 2) * batch
                xc = xc + win_scr[n, off:off + tm, :] * cw_ref[j:j + 1, lanes]
            xb = xc.astype(BF16)
            gr = jnp.dot(xb, wa_ref[r * RNN_BLOCKS + n], preferred_element_type=F32) + ba_ref[r][:, lanes]
            gi = jnp.dot(xb, wi_ref[r * RNN_BLOCKS + n], preferred_element_type=F32) + bi_ref[r][:, lanes]
            lam = lam_ref[r][:, lanes]
            softplus_neg = jnp.maximum(-lam, 0.0) + jnp.log(1.0 + jnp.exp(-jnp.abs(lam)))
            a = jnp.exp2(_sigmoid(gr) * (softplus_neg * (-LRU_C * LOG2E)))
            a_scr[n] = a
            gap = 1.0 - a * a
            b_scr[n] = gap * lax.rsqrt(jnp.maximum(gap, 1e-30)) * _sigmoid(gi) * xc
        h = [h_scr[r, :, n * blk_w:(n + 1) * blk_w] for n in range(RNN_BLOCKS)]
        order = range(tt) if r == 0 else range(tt - 1, -1, -1)
        for j in order:
            rows = pl.ds(j * batch, batch)
            for n in range(RNN_BLOCKS):
                h[n] = a_scr[n, rows, :] * h[n] + b_scr[n, rows, :]
                y_scr[n, rows, :] = h[n]
        for n in range(RNN_BLOCKS):
            h_scr[r, :, n * blk_w:(n + 1) * blk_w] = h[n]
        for b in range(batch):
            for n in range(RNN_BLOCKS):
                y_ref[b, :, n * blk_w:(n + 1) * blk_w] = (
                    y_scr[n, pl.ds(b, tt, stride=batch), :].astype(y_ref.dtype))

    direction(0, pf_ref, cf_ref, nf_ref, k > 0, k < nk - 1, yf_ref)
    direction(1, pb_ref, cb_ref, nb_ref, k < nk - 1, k > 0, yb_ref)

    @pl.when(k == nk - 1)
    def _():
        hfin_ref[...] = h_scr[...]


def _lru_call(y3, h0, conv_w, conv_b, wa, wi, ba, bi, lam):
    batch, t_seq, _ = y3.shape
    w = 1024
    tt = 64
    halo = BF16_ROWS
    per = tt // halo
    nk = t_seq // tt
    last = t_seq // halo - 1

    def specs(chunk):
        return [pl.BlockSpec((batch, halo, w), lambda s: (0, jnp.maximum(chunk(s) * per - 1, 0), COL_RX)),
                pl.BlockSpec((batch, tt, w), lambda s: (0, chunk(s), COL_RX)),
                pl.BlockSpec((batch, halo, w), lambda s: (0, jnp.minimum((chunk(s) + 1) * per, last), COL_RX))]

    fwd = lambda s: s
    bwd = lambda s: nk - 1 - s
    params = [conv_w, conv_b.reshape(1, w), wa, wi, ba.reshape(2, 1, w), bi.reshape(2, 1, w),
              lam.reshape(2, 1, w)]
    tm = tt * batch
    return pl.pallas_call(
        _lru_kernel,
        grid=(nk,),
        in_specs=specs(fwd) + specs(bwd) + [_full(h0, 1)] + [_full(a, 1) for a in params],
        out_specs=[pl.BlockSpec((batch, tt, w), lambda s: (0, fwd(s), 0)),
                   pl.BlockSpec((batch, tt, w), lambda s: (0, bwd(s), 0)),
                   pl.BlockSpec((2, batch, w), lambda s: (0, 0, 0))],
        out_shape=[jax.ShapeDtypeStruct((batch, t_seq, w), BF16)] * 2
                  + [jax.ShapeDtypeStruct((2, batch, w), F32)],
        scratch_shapes=[pltpu.VMEM((2, batch, w), F32),
                        pltpu.VMEM((RNN_BLOCKS, (tt + 2 * halo) * batch, w // RNN_BLOCKS), F32)]
                       + [pltpu.VMEM((RNN_BLOCKS, tm, w // RNN_BLOCKS), F32)] * 3,
        compiler_params=_cparams(1),
        name="conv_rglru_scan",
    )(y3, y3, y3, y3, y3, y3, h0, *params)


KEY_CHUNK = 512
ROW_STRIP = 64


def _key_chunks(kv_pieces):
    chunks = []
    col = 0
    for k_ref, v_ref in kv_pieces:
        n = k_ref.shape[0]
        kc = min(KEY_CHUNK, n)
        for r0 in range(0, n, kc):
            chunks.append((k_ref, v_ref, r0, col, kc))
            col += kc
    return chunks


def _lane_tiles(c0, kc):
    return [slice(c0 + i * LANES, c0 + (i + 1) * LANES) for i in range(kc // LANES)]


def _row_strips(tq):
    return [slice(r * ROW_STRIP, (r + 1) * ROW_STRIP) for r in range(tq // ROW_STRIP)]


def _pv_chunk(p_scr, chunk, vcols):
    _, v_ref, r0, c0, kc = chunk
    return jnp.dot(p_scr[:, c0:c0 + kc], v_ref[r0:r0 + kc, vcols], preferred_element_type=F32)


def _score_exp(q, chunks, cols, s_scr, p_scr, want_sum):
    tq = q.shape[0]
    for k_ref, _, r0, c0, kc in chunks:
        s_scr[:, c0:c0 + kc] = lax.dot_general(q, k_ref[r0:r0 + kc, cols], (((1,), (1,)), ((), ())),
                                               preferred_element_type=F32)
    tiles = [lanes for chunk in chunks for lanes in _lane_tiles(chunk[3], chunk[4])]
    sums = []
    for rows in _row_strips(tq):
        m = functools.reduce(jnp.maximum, [s_scr[rows, lanes] for lanes in tiles])
        m = jnp.broadcast_to(jnp.max(m, axis=-1, keepdims=True), (ROW_STRIP, LANES))
        lpart = None
        for lanes in tiles:
            e = jnp.exp2(s_scr[rows, lanes] - m)
            if want_sum:
                lpart = e if lpart is None else lpart + e
            p_scr[rows, lanes] = e.astype(BF16)
        if want_sum:
            sums.append(jnp.sum(lpart, axis=-1, keepdims=True))
    return jnp.concatenate(sums, axis=0) if want_sum else None


def _softmax_pv(q, kv_pieces, cols, vcols, s_scr, p_scr, sum_lane=None):
    chunks = _key_chunks(kv_pieces)
    l = _score_exp(q, chunks, cols, s_scr, p_scr, want_sum=sum_lane is None)
    acc = functools.reduce(jnp.add, [_pv_chunk(p_scr, chunk, vcols) for chunk in chunks])
    if sum_lane is not None:
        l = acc[:, sum_lane:sum_lane + 1]
    return acc * (1.0 / l)


class _UnitRunner:
    def __init__(self, n_units, tq, kv, s_scr, p_scr, sum_lane=None):
        self.n_units, self.tq, self.kv, self.s_scr, self.p_scr = n_units, tq, kv, s_scr, p_scr
        self.sum_lane = sum_lane
        self.done = 0
        self.pieces = 0

    def run(self, q_of_rows, cols, vcols):
        edge = self.done in (0, self.n_units - 1)
        self.done += 1
        half = self.tq // 2
        if edge and half % ROW_STRIP == 0:
            row_sets = [pl.ds(0, half), pl.ds(half, half)]
        else:
            row_sets = [pl.ds(0, self.tq)]
        outs = []
        for rows in row_sets:
            buf = self.pieces % 2
            self.pieces += 1
            outs.append(_softmax_pv(q_of_rows(rows), self.kv, cols, vcols,
                                    self.s_scr.at[buf, rows], self.p_scr.at[buf, rows], self.sum_lane))
        return outs[0] if len(outs) == 1 else jnp.concatenate(outs, axis=0)


def _mla_attn_kernel(q_ref, *refs):
    o_ref, s_scr, p_scr = refs[-3:]
    kv = [(refs[i], refs[i + 1]) for i in range(0, len(refs) - 3, 2)]
    tq = q_ref.shape[0]
    n_heads = q_ref.shape[1] // LANES
    lane = lax.broadcasted_iota(jnp.int32, (tq, LANES), 1)
    units = _UnitRunner(n_heads, tq, kv, s_scr, p_scr, sum_lane=MLA_V)
    for p in range(n_heads // 2):
        outs = []
        for hh in (2 * p, 2 * p + 1):
            cols = slice(hh * LANES, (hh + 1) * LANES)
            outs.append(units.run(lambda rows, cols=cols: q_ref[rows, cols], cols, cols))
        pair = jnp.where(lane < MLA_V, outs[0], pltpu.roll(outs[1], MLA_V, axis=1))
        o_ref[:, p * LANES:(p + 1) * LANES] = pair.astype(o_ref.dtype)


def _diff_attn_kernel(dl_ref, q_ref, *refs, lam_init):
    o_ref, s_scr, p_scr = refs[-3:]
    kv = [(refs[i], refs[i + 1]) for i in range(0, len(refs) - 3, 2)]
    dl = dl_ref[...]
    lam = (jnp.exp(jnp.sum(dl[0:1] * dl[1:2], axis=-1, keepdims=True))
           - jnp.exp(jnp.sum(dl[2:3] * dl[3:4], axis=-1, keepdims=True)) + lam_init)
    n_heads = q_ref.shape[1] // LANES
    units = _UnitRunner(2 * n_heads, q_ref.shape[0], kv, s_scr, p_scr, sum_lane=DIFF_V)
    for h in range(n_heads):
        cols = slice(h * LANES, (h + 1) * LANES)
        vcols = slice(2 * h * LANES, (2 * h + 2) * LANES)

        def map_q(rows, first, cols=cols):
            q = q_ref[rows, cols].astype(F32)
            in_first = lax.broadcasted_iota(jnp.int32, q.shape, 1) < DIFF_HD
            return (jnp.where(in_first, q, 0.0) if first else jnp.where(in_first, 0.0, q)).astype(BF16)

        o0 = units.run(functools.partial(map_q, first=True), cols, vcols)[:, :DIFF_V]
        o1 = units.run(functools.partial(map_q, first=False), cols, vcols)[:, :DIFF_V]
        o_ref[:, cols] = (o0 - lam * o1).astype(o_ref.dtype)


def _attn_call(kernel, extra, q, kv_arrays, *, batch, t_q, tq, heads_per_step, qw, vw, ow, n_heads, vcol0s,
               n_pbuf, name):
    groups = n_heads // heads_per_step
    nq = t_q // tq
    n_keys = sum(t_kv for _, _, t_kv in kv_arrays)
    kv_specs, kv_args = [], []
    for (k, v, t_kv), vcol0 in zip(kv_arrays, vcol0s):
        kv_specs += [pl.BlockSpec((t_kv, heads_per_step * qw), lambda b, g, i: (b, g)),
                     pl.BlockSpec((t_kv, heads_per_step * vw), lambda b, g, i, c=vcol0: (b, c + g))]
        kv_args += [k, v]
    return pl.pallas_call(
        kernel,
        grid=(batch, groups, nq),
        in_specs=[_full(a, 3) for a in extra]
                 + [pl.BlockSpec((tq, heads_per_step * qw), lambda b, g, i: (b * nq + i, g))] + kv_specs,
        out_specs=pl.BlockSpec((tq, heads_per_step * ow), lambda b, g, i: (b * nq + i, g)),
        out_shape=jax.ShapeDtypeStruct((batch * t_q, n_heads * ow), BF16),
        scratch_shapes=[pltpu.VMEM((2, tq, n_keys), F32), pltpu.VMEM((n_pbuf, tq, n_keys), BF16)],
        compiler_params=_cparams(3),
        name=name,
    )(*extra, q, *kv_args)


def _merge_kernel(x_ref, yf_ref, yb_ref, rg_ref, ob_ref, oc_ref, mg_ref, g1_ref, sub_g_ref,
                  wa_ref, wb_ref, wc_ref, wo_ref, o_ref, *, lam_init):
    d = x_ref.shape[-1]
    ya = yf_ref[...].astype(F32) + yb_ref[...].astype(F32)
    za = (ya * _gelu_tanh(rg_ref[...].astype(F32))).astype(BF16)
    br_a = jnp.dot(za, wa_ref[...], preferred_element_type=F32)
    br_b = jnp.dot(ob_ref[...], wb_ref[...], preferred_element_type=F32)
    oc = oc_ref[...].astype(F32)
    sub_gain = sub_g_ref[...] * (1.0 - lam_init)
    oc_n = jnp.concatenate(
        [_rms(oc[:, h * DIFF_V:(h + 1) * DIFF_V], DIFF_V) * sub_gain for h in range(DIFF_HEADS)], axis=1)
    br_c = jnp.dot(oc_n.astype(BF16), wc_ref[...], preferred_element_type=F32)
    mg = mg_ref[...].astype(F32)
    mix = (_sigmoid(mg[:, :d]) * br_a + _sigmoid(mg[:, d:2 * d]) * br_b
           + _sigmoid(mg[:, 2 * d:]) * br_c)
    m = jnp.dot(mix.astype(BF16), wo_ref[...], preferred_element_type=F32)
    o_ref[...] = x_ref[...] + g1_ref[...] * m


def _merge_call(xs, yf, yb, y, ob, oc, g1, sub_g, wa, wb, wc, wo, lam_init):
    rows, d = xs.shape
    tm = 512
    per_seq = rows // tm // g1.shape[0]
    blk = lambda: pl.BlockSpec((tm, d), lambda i: (i, 0))
    return pl.pallas_call(
        functools.partial(_merge_kernel, lam_init=lam_init),
        grid=(rows // tm,),
        in_specs=[blk(), blk(), blk(),
                  pl.BlockSpec((tm, d), lambda i: (i, COL_RG)),
                  blk(), blk(),
                  pl.BlockSpec((tm, N_BRANCH * d), lambda i: (i, COL_MG // N_BRANCH)),
                  pl.BlockSpec((None, 1, d), lambda i: (i // per_seq, 0, 0)),
                  _full(sub_g, 1), _full(wa, 1), _full(wb, 1), _full(wc, 1), _full(wo, 1)],
        out_specs=blk(),
        out_shape=jax.ShapeDtypeStruct((rows, d), F32),
        compiler_params=_cparams(1),
        name="branch_merge",
    )(xs, yf, yb, y, ob, oc, y, g1, sub_g, wa, wb, wc, wo)


def _ffn_kernel(x_ref, g_ref, sc_ref, sh_ref, g2_ref, wg_ref, wu_ref, wout_ref, o_ref, h_scr, acc_scr):
    j = pl.program_id(1)

    @pl.when(j == 0)
    def _():
        xn = _rms(x_ref[...], x_ref.shape[-1]) * g_ref[...]
        h_scr[...] = (xn * (1.0 + sc_ref[...]) + sh_ref[...]).astype(BF16)
        acc_scr[...] = jnp.zeros_like(acc_scr)

    h = h_scr[...]
    gate = jnp.dot(h, wg_ref[...], preferred_element_type=F32)
    up = jnp.dot(h, wu_ref[...], preferred_element_type=F32)
    act = (_silu(gate) * up).astype(BF16)
    acc_scr[...] += jnp.dot(act, wout_ref[...], preferred_element_type=F32)

    @pl.when(j == pl.num_programs(1) - 1)
    def _():
        o_ref[...] = x_ref[...] + g2_ref[...] * acc_scr[...]


def _ffn_call(xs, norm_g, sc, sh, g2, w_in, w_out, th):
    rows, d = xs.shape
    hidden = w_out.shape[0]
    nj = hidden // th
    tm = 512
    per_seq = rows // tm // sc.shape[0]
    seq = lambda i, j: (i // per_seq, 0, 0)
    return pl.pallas_call(
        _ffn_kernel,
        grid=(rows // tm, nj),
        in_specs=[pl.BlockSpec((tm, d), lambda i, j: (i, 0)),
                  pl.BlockSpec((1, d), lambda i, j: (0, 0)),
                  pl.BlockSpec((None, 1, d), seq),
                  pl.BlockSpec((None, 1, d), seq),
                  pl.BlockSpec((None, 1, d), seq),
                  pl.BlockSpec((d, th), lambda i, j: (0, j)),
                  pl.BlockSpec((d, th), lambda i, j: (0, nj + j)),
                  pl.BlockSpec((th, d), lambda i, j: (j, 0))],
        out_specs=pl.BlockSpec((tm, d), lambda i, j: (i, 0)),
        out_shape=jax.ShapeDtypeStruct((rows, d), F32),
        scratch_shapes=[pltpu.VMEM((tm, d), BF16), pltpu.VMEM((tm, d), F32)],
        compiler_params=_cparams(2),
        name="norm_swiglu_ffn",
    )(xs, norm_g.reshape(1, d), sc, sh, g2, w_in, w_in, w_out)


def _rope_lane_tables(t_lat, rot_dim, lane_of_pair, copies):
    rows_n = t_lat // GRID_W
    row_ids = jnp.repeat(jnp.arange(rows_n, dtype=F32), GRID_W)
    col_ids = jnp.tile(jnp.arange(GRID_W, dtype=F32), rows_n)
    n = rot_dim // 4
    freqs = ROPE_BASE ** (-jnp.arange(n, dtype=F32) / n)
    ang = jnp.concatenate([row_ids[:, None] * freqs, col_ids[:, None] * freqs], axis=-1)
    cos, sin = jnp.cos(ang), jnp.sin(ang)
    npairs = rot_dim // 2
    sel_c = np.zeros((npairs, LANES), np.float32)
    sel_s = np.zeros((npairs, LANES), np.float32)
    base_c = np.ones((LANES,), np.float32)
    for off in copies:
        for i in range(npairs):
            lane = off + lane_of_pair(i)
            sel_c[i, lane] = sel_c[i, lane + 1] = 1.0
            sel_s[i, lane] = -1.0
            sel_s[i, lane + 1] = 1.0
            base_c[lane] = base_c[lane + 1] = 0.0
    pick_c = np.argmax(sel_c, axis=0)
    pick_s = np.argmax(np.abs(sel_s), axis=0)
    c_tab = jnp.where(jnp.asarray(base_c > 0)[None, :], 1.0, cos[:, pick_c])
    s_tab = sin[:, pick_s] * jnp.asarray(sel_s.sum(axis=0))[None, :]
    return c_tab, s_tab


def _pair_swap_matrix(rot_dim, lane_of_pair, copies):
    p = np.zeros((LANES, LANES), np.float32)
    for off in copies:
        for i in range(rot_dim // 2):
            lane = off + lane_of_pair(i)
            p[lane + 1, lane] = 1.0
            p[lane, lane + 1] = 1.0
    return jnp.asarray(p, BF16)


def kernel(x, c, ctx, c_ctx, w_mod, b_mod, norm1_g, norm2_g, w_in, conv_w, conv_b, lru_wa, lru_ba, lru_wi, lru_bi, lru_lambda, mla_qn_g, mla_w_uq, mla_kvn_g, mla_w_ukv, mla_q_g, mla_k_g, diff_q_g, diff_k_g, diff_lambda, diff_subln_g, w_br_a, w_br_b, w_br_c, w_out, w_ffn_in, w_ffn_out):
    batch, t_lat, d = x.shape
    t_ctx = ctx.shape[1]
    depth = w_mod.shape[0]
    assert batch == SUBLANES and d == 1024
    q_rank = mla_qn_g.shape[1]
    kv_rank = mla_kvn_g.shape[1]
    hidden = w_ffn_out.shape[1]
    th = hidden // 2
    head_dim = MLA_NOPE + MLA_ROPE

    xl = x.reshape(batch * t_lat, d)
    xc = ctx.reshape(batch * t_ctx, d)

    cc = jnp.concatenate([c, c_ctx[None, :], jnp.zeros((BF16_ROWS - batch - 1, d), F32)], axis=0)
    mod = _mod_call(cc, w_mod, b_mod)

    mla_pair_lane = lambda i: MLA_NOPE + 2 * i
    diff_pair_lane = lambda i: 2 * i
    tabs = (_rope_lane_tables(t_lat, MLA_ROPE, mla_pair_lane, (0,))
            + _rope_lane_tables(t_lat, DIFF_HD, diff_pair_lane, (0, DIFF_HD)))
    perms = (_pair_swap_matrix(MLA_ROPE, mla_pair_lane, (0,)),
             _pair_swap_matrix(DIFF_HD, diff_pair_lane, (0, DIFF_HD)))
    ones_np = np.zeros((LANES, LANES), np.float32)
    ones_np[:DIFF_HD, :DIFF_HD] = 1.0
    ones_np[DIFF_HD:, DIFF_HD:] = 1.0
    ones_blk = jnp.asarray(ones_np, BF16)

    offs = np.cumsum([0, d, d, q_rank, kv_rank, MLA_ROPE, d, d, d, N_BRANCH * d])
    o_rx, o_rg, o_cq, o_ckv, o_kr, o_dq, o_dk, o_dv, o_mg = offs[:9]
    blk_w = d // RNN_BLOCKS

    for l in range(depth):
        need_ctx = l < depth - 1
        lam_init = 0.8 - 0.6 * math.exp(-0.3 * l)
        wl = w_in[l].astype(BF16)
        mla_cols = jnp.concatenate(
            [wl[:, o_cq:o_cq + q_rank], wl[:, o_ckv:o_ckv + kv_rank],
             jnp.zeros((d, MLA_NOPE), BF16), wl[:, o_kr:o_kr + MLA_ROPE],
             jnp.zeros((d, d - q_rank - kv_rank - head_dim), BF16)], axis=1)
        w_in_p = jnp.concatenate(
            [mla_cols, wl[:, o_rx:o_rx + d], wl[:, o_rg:o_rg + d], wl[:, o_dq:o_dq + d],
             wl[:, o_dk:o_dk + d], wl[:, o_dv:o_dv + d], wl[:, o_mg:o_mg + N_BRANCH * d]],
            axis=1)
        w_uq_p = jnp.pad(mla_w_uq[l].reshape(q_rank, MLA_HEADS, head_dim),
                         ((0, 0), (0, 0), (0, LANES - head_dim))).reshape(q_rank, MLA_HEADS * LANES).astype(BF16)
        w_ukv = mla_w_ukv[l].reshape(kv_rank, MLA_HEADS, MLA_NOPE + MLA_V)
        w_k_p = jnp.pad(w_ukv[:, :, :MLA_NOPE],
                        ((0, 0), (0, 0), (0, LANES - MLA_NOPE))).reshape(kv_rank, MLA_HEADS * LANES).astype(BF16)
        w_v_p = jnp.pad(w_ukv[:, :, MLA_NOPE:],
                        ((0, 0), (0, 0), (0, LANES - MLA_V))).reshape(kv_rank, MLA_HEADS * LANES).astype(BF16)
        pad_gain = lambda g: jnp.pad(g, (0, LANES - head_dim)).reshape(1, LANES)
        gains = [mla_qn_g[l].reshape(1, q_rank), mla_kvn_g[l].reshape(1, kv_rank),
                 pad_gain(mla_q_g[l]), pad_gain(mla_k_g[l]),
                 jnp.tile(diff_q_g[l], 2).reshape(1, LANES), jnp.tile(diff_k_g[l], 2).reshape(1, LANES)]
        prep_w = (w_uq_p, w_k_p, w_v_p)
        w_ffn_in_p = w_ffn_in[l].astype(BF16)
        w_ffn_out_p = w_ffn_out[l].astype(BF16)
        lru_params = (conv_w[l], conv_b[l],
                      lru_wa[l].reshape(2 * RNN_BLOCKS, blk_w, blk_w).astype(BF16),
                      lru_wi[l].reshape(2 * RNN_BLOCKS, blk_w, blk_w).astype(BF16),
                      lru_ba[l], lru_bi[l], lru_lambda[l])
        merge_w = (diff_subln_g[l].reshape(1, DIFF_V), w_br_a[l].astype(BF16), w_br_b[l].astype(BF16),
                   w_br_c[l].astype(BF16), w_out[l].astype(BF16))

        mod6 = mod[l].reshape(mod.shape[1], 6, d)
        lat_mod = [mod6[:batch, i].reshape(batch, 1, d) for i in range(6)]
        ctx_mod = [mod6[batch:batch + 1, i].reshape(1, 1, d) for i in range(6)]

        yc = _inproj_call(xc, norm1_g[l], ctx_mod[1], ctx_mod[0], w_in_p)
        yc3 = yc.reshape(batch, t_ctx, yc.shape[1])
        yfc, ybc, h_ctx = _lru_call(yc3, jnp.zeros((2, batch, d), F32), *lru_params)
        qmc, kmc, vmc, qdc, kdc, vdc = _prep_call(yc, (), gains, prep_w, (ones_blk,), t_ctx)

        yl = _inproj_call(xl, norm1_g[l], lat_mod[1], lat_mod[0], w_in_p)
        yl3 = yl.reshape(batch, t_lat, yl.shape[1])
        yfl, ybl, _ = _lru_call(yl3, h_ctx, *lru_params)
        qml, kml, vml, qdl, kdl, vdl = _prep_call(yl, tabs, gains, prep_w, (ones_blk,) + perms, t_lat)

        mla_hps, diff_hps = 8, 4
        mla_kw = dict(batch=batch, heads_per_step=mla_hps, qw=LANES, vw=LANES, ow=MLA_V, n_heads=MLA_HEADS,
                      n_pbuf=2)
        diff_kw = dict(batch=batch, heads_per_step=diff_hps, qw=LANES, vw=2 * DIFF_V, ow=DIFF_V,
                       n_heads=DIFF_HEADS, n_pbuf=2)
        diff_kernel = functools.partial(_diff_attn_kernel, lam_init=lam_init)
        tq = 512
        o_b = _attn_call(_mla_attn_kernel, (), qml, [(kml, vml, t_lat), (kmc, vmc, t_ctx)],
                         t_q=t_lat, tq=tq, vcol0s=(0, 0), name="mla_attention", **mla_kw)
        o_c = _attn_call(diff_kernel, (diff_lambda[l],), qdl, [(kdl, vdl, t_lat), (kdc, vdc, t_ctx)],
                         t_q=t_lat, tq=tq, vcol0s=(0, 0), name="diff_attention", **diff_kw)
        xl_new = _merge_call(xl, yfl.reshape(-1, d), ybl.reshape(-1, d), yl, o_b, o_c, lat_mod[2],
                             *merge_w, lam_init)
        xl = _ffn_call(xl_new, norm2_g[l], lat_mod[4], lat_mod[3], lat_mod[5], w_ffn_in_p, w_ffn_out_p, th)

        if need_ctx:
            o_bc = _attn_call(_mla_attn_kernel, (), qmc, [(kmc, vmc, t_ctx)],
                              t_q=t_ctx, tq=t_ctx, vcol0s=(0,), name="mla_attention_ctx", **mla_kw)
            o_cc = _attn_call(diff_kernel, (diff_lambda[l],), qdc, [(kdc, vdc, t_ctx)],
                              t_q=t_ctx, tq=t_ctx, vcol0s=(0,), name="diff_attention_ctx", **diff_kw)
            xc_new = _merge_call(xc, yfc.reshape(-1, d), ybc.reshape(-1, d), yc, o_bc, o_cc, ctx_mod[2],
                                 *merge_w, lam_init)
            xc = _ffn_call(xc_new, norm2_g[l], ctx_mod[4], ctx_mod[3], ctx_mod[5], w_ffn_in_p, w_ffn_out_p, th)

    return xl.reshape(batch, t_lat, d)
```

```python
import functools
import math

import numpy as np
import jax
import jax.numpy as jnp
from jax import lax
from jax.experimental import pallas as pl
from jax.experimental.pallas import tpu as pltpu

F32 = jnp.float32
BF16 = jnp.bfloat16

EPS = 1e-6
GRID_W = 64
ROPE_BASE = 10000.0
LRU_C = 8.0
CONV_W = 4
RNN_BLOCKS = 8
MLA_HEADS = 16
MLA_NOPE = 64
MLA_ROPE = 32
MLA_V = 64
MLA_SCALE = (MLA_NOPE + MLA_ROPE) ** -0.5
DIFF_HEADS = 8
DIFF_HD = 64
DIFF_V = 2 * DIFF_HD
DIFF_SCALE = DIFF_HD ** -0.5
N_BRANCH = 3
LOG2E = math.log2(math.e)

LANES = 128
SUBLANES = 8
BF16_ROWS = 16
VMEM_LIMIT = 56 * 1024 * 1024

COL_MLA, COL_RX, COL_RG, COL_DQ, COL_DK, COL_DV, COL_MG = 0, 1, 2, 3, 4, 5, 6


def _cparams(n_axes, flags=None):
    return pltpu.CompilerParams(dimension_semantics=("arbitrary",) * n_axes,
                                vmem_limit_bytes=VMEM_LIMIT, flags=flags)


def _full(a, n_axes):
    return pl.BlockSpec(a.shape, lambda *_: (0,) * a.ndim)


def _sigmoid(x):
    return 0.5 * jnp.tanh(0.5 * x) + 0.5


def _silu(x):
    return x * _sigmoid(x)


def _gelu_tanh(x):
    return 0.5 * x * (1.0 + jnp.tanh(math.sqrt(2.0 / math.pi) * (x + 0.044715 * (x * x * x))))


def _rms(x, denom):
    ms = jnp.sum(x * x, axis=-1, keepdims=True) * (1.0 / denom)
    return x * lax.rsqrt(ms + EPS)


def _mod_kernel(c_ref, w_ref, b_ref, o_ref):
    s = _silu(c_ref[...]).astype(BF16)
    o_ref[...] = jnp.dot(s, w_ref[...].astype(BF16), preferred_element_type=F32) + b_ref[...]


def _mod_call(cc, w_mod, b_mod):
    depth, d, n = w_mod.shape
    tn = 1024
    return pl.pallas_call(
        _mod_kernel,
        grid=(depth, n // tn),
        in_specs=[pl.BlockSpec((cc.shape[0], d), lambda l, j: (0, 0)),
                  pl.BlockSpec((None, d, tn), lambda l, j: (l, 0, j)),
                  pl.BlockSpec((None, 1, tn), lambda l, j: (l, 0, j))],
        out_specs=pl.BlockSpec((None, cc.shape[0], tn), lambda l, j: (l, 0, j)),
        out_shape=jax.ShapeDtypeStruct((depth, cc.shape[0], n), F32),
        compiler_params=_cparams(2),
        name="adaln_mod",
    )(cc, w_mod, b_mod.reshape(depth, 1, n))


def _inproj_kernel(x_ref, g_ref, sc_ref, sh_ref, w_ref, o_ref, h_scr):
    @pl.when(pl.program_id(1) == 0)
    def _():
        xn = _rms(x_ref[...], x_ref.shape[-1]) * g_ref[...]
        h_scr[...] = (xn * (1.0 + sc_ref[...]) + sh_ref[...]).astype(BF16)

    o_ref[...] = jnp.dot(h_scr[...], w_ref[...], preferred_element_type=F32).astype(o_ref.dtype)


def _inproj_call(xs, norm_g, sc, sh, w_in_p):
    rows, d = xs.shape
    n = w_in_p.shape[1]
    tm, tn = min(1024, rows // sc.shape[0]), 2304
    per_seq = rows // tm // sc.shape[0]
    seq = lambda i, j: (i // per_seq, 0, 0)
    return pl.pallas_call(
        _inproj_kernel,
        grid=(rows // tm, n // tn),
        in_specs=[pl.BlockSpec((tm, d), lambda i, j: (i, 0)),
                  pl.BlockSpec((1, d), lambda i, j: (0, 0)),
                  pl.BlockSpec((None, 1, d), seq),
                  pl.BlockSpec((None, 1, d), seq),
                  pl.BlockSpec((d, tn), lambda i, j: (0, j))],
        out_specs=pl.BlockSpec((tm, tn), lambda i, j: (i, j)),
        out_shape=jax.ShapeDtypeStruct((rows, n), BF16),
        scratch_shapes=[pltpu.VMEM((tm, d), BF16)],
        compiler_params=_cparams(2),
        name="norm_inproj",
    )(xs, norm_g.reshape(1, d), sc, sh, w_in_p)


def _prep_kernel(*refs, use_rope):
    if use_rope:
        (ymla_ref, ydq_ref, ydk_ref, ydv_ref, cm_ref, sm_ref, cd_ref, sd_ref,
         qn_g_ref, kvn_g_ref, q_g_ref, k_g_ref, dq_g_ref, dk_g_ref,
         wuq_ref, wk_ref, wv_ref, ones_ref, pm_ref, pd_ref,
         qm_ref, km_ref, vm_ref, qd_ref, kd_ref, vd_ref) = refs
        cm, sm, cd, sd = cm_ref[...], sm_ref[...], cd_ref[...], sd_ref[...]
        pm, pd = pm_ref[...], pd_ref[...]
    else:
        (ymla_ref, ydq_ref, ydk_ref, ydv_ref,
         qn_g_ref, kvn_g_ref, q_g_ref, k_g_ref, dq_g_ref, dk_g_ref,
         wuq_ref, wk_ref, wv_ref, ones_ref,
         qm_ref, km_ref, vm_ref, qd_ref, kd_ref, vd_ref) = refs
        cm = sm = cd = sd = pm = pd = None
    q_rank = qn_g_ref.shape[-1]
    kv_rank = kvn_g_ref.shape[-1]
    ymla = ymla_ref[...].astype(F32)
    cq = ymla[:, :q_rank]
    ckv = ymla[:, q_rank:q_rank + kv_rank]
    kr = ymla[:, q_rank + kv_rank:q_rank + kv_rank + LANES]
    head_dim = MLA_NOPE + MLA_ROPE

    def rope(u, cos, sin, perm):
        if not use_rope:
            return u
        return u * cos + jnp.dot(u.astype(BF16), perm, preferred_element_type=F32) * sin

    cqn = (_rms(cq, q_rank) * qn_g_ref[...]).astype(BF16)
    q = jnp.dot(cqn, wuq_ref[...], preferred_element_type=F32)
    q_gain = q_g_ref[...] * (MLA_SCALE * LOG2E)
    for h in range(MLA_HEADS):
        u = _rms(q[:, h * LANES:(h + 1) * LANES], head_dim) * q_gain
        qm_ref[:, h * LANES:(h + 1) * LANES] = rope(u, cm, sm, pm).astype(BF16)

    ckvn = (_rms(ckv, kv_rank) * kvn_g_ref[...]).astype(BF16)
    vm = jnp.dot(ckvn, wv_ref[...], preferred_element_type=F32)
    sum_lane = (lax.broadcasted_iota(jnp.int32, vm.shape, 1) & (LANES - 1)) == MLA_V
    vm_ref[...] = jnp.where(sum_lane, 1.0, vm).astype(BF16)
    kn = jnp.dot(ckvn, wk_ref[...], preferred_element_type=F32)
    k_gain = k_g_ref[...]
    for h in range(MLA_HEADS):
        u = _rms(kn[:, h * LANES:(h + 1) * LANES] + kr, head_dim) * k_gain
        km_ref[:, h * LANES:(h + 1) * LANES] = rope(u, cm, sm, pm).astype(BF16)

    ones_blk = ones_ref[...]

    def diff_norm_rope(y_ref, gain, o_ref):
        for h in range(DIFF_HEADS):
            x = y_ref[:, h * LANES:(h + 1) * LANES].astype(F32)
            ms = jnp.dot((x * x).astype(BF16), ones_blk, preferred_element_type=F32) * (1.0 / DIFF_HD)
            u = x * lax.rsqrt(ms + EPS) * gain
            o_ref[:, h * LANES:(h + 1) * LANES] = rope(u, cd, sd, pd).astype(BF16)

    diff_norm_rope(ydq_ref, dq_g_ref[...] * (DIFF_SCALE * LOG2E), qd_ref)
    diff_norm_rope(ydk_ref, dk_g_ref[...], kd_ref)

    tm = ydv_ref.shape[0]
    one_hot = (lax.broadcasted_iota(jnp.int32, (tm, LANES), 1) == 0).astype(BF16)
    for h in range(DIFF_HEADS):
        vd_ref[:, 2 * h * LANES:(2 * h + 1) * LANES] = ydv_ref[:, h * LANES:(h + 1) * LANES]
        vd_ref[:, (2 * h + 1) * LANES:(2 * h + 2) * LANES] = one_hot


def _prep_call(y, tabs, gains, weights, consts, t_seq):
    rows = y.shape[0]
    d = 1024
    tm = 256
    use_rope = len(tabs) > 0
    per_seq = t_seq // tm
    row_blk = lambda w: pl.BlockSpec((tm, w), lambda i: (i, 0))
    col_blk = lambda c: pl.BlockSpec((tm, d), lambda i, c=c: (i, c))
    tab_blk = pl.BlockSpec((tm, LANES), lambda i: (i % per_seq, 0))
    wq = MLA_HEADS * LANES
    out_shapes = [jax.ShapeDtypeStruct((rows, wq), BF16), jax.ShapeDtypeStruct((rows, wq), BF16),
                  jax.ShapeDtypeStruct((rows, wq), BF16),
                  jax.ShapeDtypeStruct((rows, d), BF16), jax.ShapeDtypeStruct((rows, d), BF16),
                  jax.ShapeDtypeStruct((rows, 2 * d), BF16)]
    small = list(gains) + list(weights) + list(consts)
    return pl.pallas_call(
        functools.partial(_prep_kernel, use_rope=use_rope),
        grid=(rows // tm,),
        in_specs=[col_blk(COL_MLA), col_blk(COL_DQ), col_blk(COL_DK), col_blk(COL_DV)]
                 + [tab_blk] * len(tabs) + [_full(a, 1) for a in small],
        out_specs=[row_blk(s.shape[1]) for s in out_shapes],
        out_shape=out_shapes,
        compiler_params=_cparams(1),
        name="attn_prep_rope" if use_rope else "attn_prep",
    )(y, y, y, y, *tabs, *small)


def _lru_kernel(pf_ref, cf_ref, nf_ref, pb_ref, cb_ref, nb_ref, h0_ref,
                cw_ref, cbias_ref, wa_ref, wi_ref, ba_ref, bi_ref, lam_ref,
                yf_ref, yb_ref, hfin_ref, h_scr, win_scr, a_scr, b_scr, y_scr):
    k = pl.program_id(0)
    nk = pl.num_programs(0)
    batch, tt, w = cf_ref.shape
    halo = pf_ref.shape[1]
    tm = tt * batch
    blk_w = w // RNN_BLOCKS

    @pl.when(k == 0)
    def _():
        h_scr[...] = h0_ref[...]

    def interleave(src_ref, row0, n_steps, keep):
        for b in range(batch):
            v = src_ref[b].astype(F32)
            if keep is not None:
                v = jnp.where(keep, v, 0.0)
            for n in range(RNN_BLOCKS):
                win_scr[n, pl.ds(row0 + b, n_steps, stride=batch), :] = v[:, n * blk_w:(n + 1) * blk_w]

    def direction(r, prev_ref, cur_ref, next_ref, has_prev, has_next, y_ref):
        interleave(prev_ref, 0, halo, has_prev)
        interleave(cur_ref, halo * batch, tt, None)
        interleave(next_ref, (halo + tt) * batch, halo, has_next)
        base = halo * batch
        for n in range(RNN_BLOCKS):
            lanes = slice(n * blk_w, (n + 1) * blk_w)
            xc = cbias_ref[:, lanes]
            for j in range(CONV_W):
                off = base + (j - CONV_W // 2) * batch
                xc = xc + win_scr[n, off:off + tm, :] * cw_ref[j:j + 1, lanes]
            xb = xc.astype(BF16)
            gr = jnp.dot(xb, wa_ref[r * RNN_BLOCKS + n], preferred_element_type=F32) + ba_ref[r][:, lanes]
            gi = jnp.dot(xb, wi_ref[r * RNN_BLOCKS + n], preferred_element_type=F32) + bi_ref[r][:, lanes]
            lam = lam_ref[r][:, lanes]
            softplus_neg = jnp.maximum(-lam, 0.0) + jnp.log(1.0 + jnp.exp(-jnp.abs(lam)))
            a = jnp.exp2(_sigmoid(gr) * (softplus_neg * (-LRU_C * LOG2E)))
            a_scr[n] = a
            gap = 1.0 - a * a
            b_scr[n] = gap * lax.rsqrt(jnp.maximum(gap, 1e-30)) * _sigmoid(gi) * xc
        h = [h_scr[r, :, n * blk_w:(n + 1) * blk_w] for n in range(RNN_BLOCKS)]
        order = range(tt) if r == 0 else range(tt - 1, -1, -1)
        for j in order:
            rows = pl.ds(j * batch, batch)
            for n in range(RNN_BLOCKS):
                h[n] = a_scr[n, rows, :] * h[n] + b_scr[n, rows, :]
                y_scr[n, rows, :] = h[n]
        for n in range(RNN_BLOCKS):
            h_scr[r, :, n * blk_w:(n + 1) * blk_w] = h[n]
        for b in range(batch):
            for n in range(RNN_BLOCKS):
                y_ref[b, :, n * blk_w:(n + 1) * blk_w] = (
                    y_scr[n, pl.ds(b, tt, stride=batch), :].astype(y_ref.dtype))

    direction(0, pf_ref, cf_ref, nf_ref, k > 0, k < nk - 1, yf_ref)
    direction(1, pb_ref, cb_ref, nb_ref, k < nk - 1, k > 0, yb_ref)

    @pl.when(k == nk - 1)
    def _():
        hfin_ref[...] = h_scr[...]


def _lru_call(y3, h0, conv_w, conv_b, wa, wi, ba, bi, lam):
    batch, t_seq, _ = y3.shape
    w = 1024
    tt = 64
    halo = BF16_ROWS
    per = tt // halo
    nk = t_seq // tt
    last = t_seq // halo - 1

    def specs(chunk):
        return [pl.BlockSpec((batch, halo, w), lambda s: (0, jnp.maximum(chunk(s) * per - 1, 0), COL_RX)),
                pl.BlockSpec((batch, tt, w), lambda s: (0, chunk(s), COL_RX)),
                pl.BlockSpec((batch, halo, w), lambda s: (0, jnp.minimum((chunk(s) + 1) * per, last), COL_RX))]

    fwd = lambda s: s
    bwd = lambda s: nk - 1 - s
    params = [conv_w, conv_b.reshape(1, w), wa, wi, ba.reshape(2, 1, w), bi.reshape(2, 1, w),
              lam.reshape(2, 1, w)]
    tm = tt * batch
    return pl.pallas_call(
        _lru_kernel,
        grid=(nk,),
        in_specs=specs(fwd) + specs(bwd) + [_full(h0, 1)] + [_full(a, 1) for a in params],
        out_specs=[pl.BlockSpec((batch, tt, w), lambda s: (0, fwd(s), 0)),
                   pl.BlockSpec((batch, tt, w), lambda s: (0, bwd(s), 0)),
                   pl.BlockSpec((2, batch, w), lambda s: (0, 0, 0))],
        out_shape=[jax.ShapeDtypeStruct((batch, t_seq, w), BF16)] * 2
                  + [jax.ShapeDtypeStruct((2, batch, w), F32)],
        scratch_shapes=[pltpu.VMEM((2, batch, w), F32),
                        pltpu.VMEM((RNN_BLOCKS, (tt + 2 * halo) * batch, w // RNN_BLOCKS), F32)]
                       + [pltpu.VMEM((RNN_BLOCKS, tm, w // RNN_BLOCKS), F32)] * 3,
        compiler_params=_cparams(1),
        name="conv_rglru_scan",
    )(y3, y3, y3, y3, y3, y3, h0, *params)


KEY_CHUNK = 512
ROW_STRIP = 64


def _key_chunks(kv_pieces):
    chunks = []
    col = 0
    for k_ref, v_ref in kv_pieces:
        n = k_ref.shape[0]
        kc = min(KEY_CHUNK, n)
        for r0 in range(0, n, kc):
            chunks.append((k_ref, v_ref, r0, col, kc))
            col += kc
    return chunks


def _lane_tiles(c0, kc):
    return [slice(c0 + i * LANES, c0 + (i + 1) * LANES) for i in range(kc // LANES)]


def _row_strips(tq):
    return [slice(r * ROW_STRIP, (r + 1) * ROW_STRIP) for r in range(tq // ROW_STRIP)]


def _pv_chunk(p_scr, chunk, vcols):
    _, v_ref, r0, c0, kc = chunk
    return jnp.dot(p_scr[:, c0:c0 + kc], v_ref[r0:r0 + kc, vcols], preferred_element_type=F32)


def _score_exp(q, chunks, cols, s_scr, p_scr, want_sum):
    tq = q.shape[0]
    for k_ref, _, r0, c0, kc in chunks:
        s_scr[:, c0:c0 + kc] = lax.dot_general(q, k_ref[r0:r0 + kc, cols], (((1,), (1,)), ((), ())),
                                               preferred_element_type=F32)
    tiles = [lanes for chunk in chunks for lanes in _lane_tiles(chunk[3], chunk[4])]
    sums = []
    for rows in _row_strips(tq):
        m = functools.reduce(jnp.maximum, [s_scr[rows, lanes] for lanes in tiles])
        m = jnp.broadcast_to(jnp.max(m, axis=-1, keepdims=True), (ROW_STRIP, LANES))
        lpart = None
        for lanes in tiles:
            e = jnp.exp2(s_scr[rows, lanes] - m)
            if want_sum:
                lpart = e if lpart is None else lpart + e
            p_scr[rows, lanes] = e.astype(BF16)
        if want_sum:
            sums.append(jnp.sum(lpart, axis=-1, keepdims=True))
    return jnp.concatenate(sums, axis=0) if want_sum else None


def _softmax_pv(q, kv_pieces, cols, vcols, s_scr, p_scr, sum_lane=None):
    chunks = _key_chunks(kv_pieces)
    l = _score_exp(q, chunks, cols, s_scr, p_scr, want_sum=sum_lane is None)
    acc = functools.reduce(jnp.add, [_pv_chunk(p_scr, chunk, vcols) for chunk in chunks])
    if sum_lane is not None:
        l = acc[:, sum_lane:sum_lane + 1]
    return acc * (1.0 / l)


class _UnitRunner:
    def __init__(self, n_units, tq, kv, s_scr, p_scr, sum_lane=None):
        self.n_units, self.tq, self.kv, self.s_scr, self.p_scr = n_units, tq, kv, s_scr, p_scr
        self.sum_lane = sum_lane
        self.done = 0
        self.pieces = 0

    def run(self, q_of_rows, cols, vcols):
        edge = self.done in (0, self.n_units - 1)
        self.done += 1
        half = self.tq // 2
        if edge and half % ROW_STRIP == 0:
            row_sets = [pl.ds(0, half), pl.ds(half, half)]
        else:
            row_sets = [pl.ds(0, self.tq)]
        outs = []
        for rows in row_sets:
            buf = self.pieces % 2
            self.pieces += 1
            outs.append(_softmax_pv(q_of_rows(rows), self.kv, cols, vcols,
                                    self.s_scr.at[buf, rows], self.p_scr.at[buf, rows], self.sum_lane))
        return outs[0] if len(outs) == 1 else jnp.concatenate(outs, axis=0)


def _mla_attn_kernel(q_ref, *refs):
    o_ref, s_scr, p_scr = refs[-3:]
    kv = [(refs[i], refs[i + 1]) for i in range(0, len(refs) - 3, 2)]
    tq = q_ref.shape[0]
    n_heads = q_ref.shape[1] // LANES
    lane = lax.broadcasted_iota(jnp.int32, (tq, LANES), 1)
    units = _UnitRunner(n_heads, tq, kv, s_scr, p_scr, sum_lane=MLA_V)
    for p in range(n_heads // 2):
        outs = []
        for hh in (2 * p, 2 * p + 1):
            cols = slice(hh * LANES, (hh + 1) * LANES)
            outs.append(units.run(lambda rows, cols=cols: q_ref[rows, cols], cols, cols))
        pair = jnp.where(lane < MLA_V, outs[0], pltpu.roll(outs[1], MLA_V, axis=1))
        o_ref[:, p * LANES:(p + 1) * LANES] = pair.astype(o_ref.dtype)


def _diff_attn_kernel(dl_ref, q_ref, *refs, lam_init):
    o_ref, s_scr, p_scr = refs[-3:]
    kv = [(refs[i], refs[i + 1]) for i in range(0, len(refs) - 3, 2)]
    dl = dl_ref[...]
    lam = (jnp.exp(jnp.sum(dl[0:1] * dl[1:2], axis=-1, keepdims=True))
           - jnp.exp(jnp.sum(dl[2:3] * dl[3:4], axis=-1, keepdims=True)) + lam_init)
    n_heads = q_ref.shape[1] // LANES
    units = _UnitRunner(2 * n_heads, q_ref.shape[0], kv, s_scr, p_scr, sum_lane=DIFF_V)
    for h in range(n_heads):
        cols = slice(h * LANES, (h + 1) * LANES)
        vcols = slice(2 * h * LANES, (2 * h + 2) * LANES)

        def map_q(rows, first, cols=cols):
            q = q_ref[rows, cols].astype(F32)
            in_first = lax.broadcasted_iota(jnp.int32, q.shape, 1) < DIFF_HD
            return (jnp.where(in_first, q, 0.0) if first else jnp.where(in_first, 0.0, q)).astype(BF16)

        o0 = units.run(functools.partial(map_q, first=True), cols, vcols)[:, :DIFF_V]
        o1 = units.run(functools.partial(map_q, first=False), cols, vcols)[:, :DIFF_V]
        o_ref[:, cols] = (o0 - lam * o1).astype(o_ref.dtype)


def _attn_call(kernel, extra, q, kv_arrays, *, batch, t_q, tq, heads_per_step, qw, vw, ow, n_heads, vcol0s,
               n_pbuf, name):
    groups = n_heads // heads_per_step
    nq = t_q // tq
    n_keys = sum(t_kv for _, _, t_kv in kv_arrays)
    kv_specs, kv_args = [], []
    for (k, v, t_kv), vcol0 in zip(kv_arrays, vcol0s):
        kv_specs += [pl.BlockSpec((t_kv, heads_per_step * qw), lambda b, g, i: (b, g)),
                     pl.BlockSpec((t_kv, heads_per_step * vw), lambda b, g, i, c=vcol0: (b, c + g))]
        kv_args += [k, v]
    return pl.pallas_call(
        kernel,
        grid=(batch, groups, nq),
        in_specs=[_full(a, 3) for a in extra]
                 + [pl.BlockSpec((tq, heads_per_step * qw), lambda b, g, i: (b * nq + i, g))] + kv_specs,
        out_specs=pl.BlockSpec((tq, heads_per_step * ow), lambda b, g, i: (b * nq + i, g)),
        out_shape=jax.ShapeDtypeStruct((batch * t_q, n_heads * ow), BF16),
        scratch_shapes=[pltpu.VMEM((2, tq, n_keys), F32), pltpu.VMEM((n_pbuf, tq, n_keys), BF16)],
        compiler_params=_cparams(3),
        name=name,
    )(*extra, q, *kv_args)


def _merge_kernel(x_ref, yf_ref, yb_ref, rg_ref, ob_ref, oc_ref, mg_ref, g1_ref, sub_g_ref,
                  wa_ref, wb_ref, wc_ref, wo_ref, o_ref, *, lam_init):
    d = x_ref.shape[-1]
    ya = yf_ref[...].astype(F32) + yb_ref[...].astype(F32)
    za = (ya * _gelu_tanh(rg_ref[...].astype(F32))).astype(BF16)
    br_a = jnp.dot(za, wa_ref[...], preferred_element_type=F32)
    br_b = jnp.dot(ob_ref[...], wb_ref[...], preferred_element_type=F32)
    oc = oc_ref[...].astype(F32)
    sub_gain = sub_g_ref[...] * (1.0 - lam_init)
    oc_n = jnp.concatenate(
        [_rms(oc[:, h * DIFF_V:(h + 1) * DIFF_V], DIFF_V) * sub_gain for h in range(DIFF_HEADS)], axis=1)
    br_c = jnp.dot(oc_n.astype(BF16), wc_ref[...], preferred_element_type=F32)
    mg = mg_ref[...].astype(F32)
    mix = (_sigmoid(mg[:, :d]) * br_a + _sigmoid(mg[:, d:2 * d]) * br_b
           + _sigmoid(mg[:, 2 * d:]) * br_c)
    m = jnp.dot(mix.astype(BF16), wo_ref[...], preferred_element_type=F32)
    o_ref[...] = x_ref[...] + g1_ref[...] * m


def _merge_call(xs, yf, yb, y, ob, oc, g1, sub_g, wa, wb, wc, wo, lam_init):
    rows, d = xs.shape
    tm = 512
    per_seq = rows // tm // g1.shape[0]
    blk = lambda: pl.BlockSpec((tm, d), lambda i: (i, 0))
    return pl.pallas_call(
        functools.partial(_merge_kernel, lam_init=lam_init),
        grid=(rows // tm,),
        in_specs=[blk(), blk(), blk(),
                  pl.BlockSpec((tm, d), lambda i: (i, COL_RG)),
                  blk(), blk(),
                  pl.BlockSpec((tm, N_BRANCH * d), lambda i: (i, COL_MG // N_BRANCH)),
                  pl.BlockSpec((None, 1, d), lambda i: (i // per_seq, 0, 0)),
                  _full(sub_g, 1), _full(wa, 1), _full(wb, 1), _full(wc, 1), _full(wo, 1)],
        out_specs=blk(),
        out_shape=jax.ShapeDtypeStruct((rows, d), F32),
        compiler_params=_cparams(1),
        name="branch_merge",
    )(xs, yf, yb, y, ob, oc, y, g1, sub_g, wa, wb, wc, wo)


def _ffn_kernel(x_ref, g_ref, sc_ref, sh_ref, g2_ref, wg_ref, wu_ref, wout_ref, o_ref, h_scr, acc_scr):
    j = pl.program_id(1)

    @pl.when(j == 0)
    def _():
        xn = _rms(x_ref[...], x_ref.shape[-1]) * g_ref[...]
        h_scr[...] = (xn * (1.0 + sc_ref[...]) + sh_ref[...]).astype(BF16)
        acc_scr[...] = jnp.zeros_like(acc_scr)

    h = h_scr[...]
    gate = jnp.dot(h, wg_ref[...], preferred_element_type=F32)
    up = jnp.dot(h, wu_ref[...], preferred_element_type=F32)
    act = (_silu(gate) * up).astype(BF16)
    acc_scr[...] += jnp.dot(act, wout_ref[...], preferred_element_type=F32)

    @pl.when(j == pl.num_programs(1) - 1)
    def _():
        o_ref[...] = x_ref[...] + g2_ref[...] * acc_scr[...]


def _ffn_call(xs, norm_g, sc, sh, g2, w_in, w_out, th):
    rows, d = xs.shape
    hidden = w_out.shape[0]
    nj = hidden // th
    tm = 512
    per_seq = rows // tm // sc.shape[0]
    seq = lambda i, j: (i // per_seq, 0, 0)
    wmode = dict(pipeline_mode=pl.Buffered(1)) if nj == 1 else {}
    return pl.pallas_call(
        _ffn_kernel,
        grid=(rows // tm, nj),
        in_specs=[pl.BlockSpec((tm, d), lambda i, j: (i, 0)),
                  pl.BlockSpec((1, d), lambda i, j: (0, 0)),
                  pl.BlockSpec((None, 1, d), seq),
                  pl.BlockSpec((None, 1, d), seq),
                  pl.BlockSpec((None, 1, d), seq),
                  pl.BlockSpec((d, th), lambda i, j: (0, j), **wmode),
                  pl.BlockSpec((d, th), lambda i, j: (0, nj + j), **wmode),
                  pl.BlockSpec((th, d), lambda i, j: (j, 0), **wmode)],
        out_specs=pl.BlockSpec((tm, d), lambda i, j: (i, 0)),
        out_shape=jax.ShapeDtypeStruct((rows, d), F32),
        scratch_shapes=[pltpu.VMEM((tm, d), BF16), pltpu.VMEM((tm, d), F32)],
        compiler_params=_cparams(2),
        name="norm_swiglu_ffn",
    )(xs, norm_g.reshape(1, d), sc, sh, g2, w_in, w_in, w_out)


def _rope_lane_tables(t_lat, rot_dim, lane_of_pair, copies):
    rows_n = t_lat // GRID_W
    row_ids = jnp.repeat(jnp.arange(rows_n, dtype=F32), GRID_W)
    col_ids = jnp.tile(jnp.arange(GRID_W, dtype=F32), rows_n)
    n = rot_dim // 4
    freqs = ROPE_BASE ** (-jnp.arange(n, dtype=F32) / n)
    ang = jnp.concatenate([row_ids[:, None] * freqs, col_ids[:, None] * freqs], axis=-1)
    cos, sin = jnp.cos(ang), jnp.sin(ang)
    npairs = rot_dim // 2
    sel_c = np.zeros((npairs, LANES), np.float32)
    sel_s = np.zeros((npairs, LANES), np.float32)
    base_c = np.ones((LANES,), np.float32)
    for off in copies:
        for i in range(npairs):
            lane = off + lane_of_pair(i)
            sel_c[i, lane] = sel_c[i, lane + 1] = 1.0
            sel_s[i, lane] = -1.0
            sel_s[i, lane + 1] = 1.0
            base_c[lane] = base_c[lane + 1] = 0.0
    pick_c = np.argmax(sel_c, axis=0)
    pick_s = np.argmax(np.abs(sel_s), axis=0)
    c_tab = jnp.where(jnp.asarray(base_c > 0)[None, :], 1.0, cos[:, pick_c])
    s_tab = sin[:, pick_s] * jnp.asarray(sel_s.sum(axis=0))[None, :]
    return c_tab, s_tab


def _pair_swap_matrix(rot_dim, lane_of_pair, copies):
    p = np.zeros((LANES, LANES), np.float32)
    for off in copies:
        for i in range(rot_dim // 2):
            lane = off + lane_of_pair(i)
            p[lane + 1, lane] = 1.0
            p[lane, lane + 1] = 1.0
    return jnp.asarray(p, BF16)


def kernel(x, c, ctx, c_ctx, w_mod, b_mod, norm1_g, norm2_g, w_in, conv_w, conv_b, lru_wa, lru_ba, lru_wi, lru_bi, lru_lambda, mla_qn_g, mla_w_uq, mla_kvn_g, mla_w_ukv, mla_q_g, mla_k_g, diff_q_g, diff_k_g, diff_lambda, diff_subln_g, w_br_a, w_br_b, w_br_c, w_out, w_ffn_in, w_ffn_out):
    batch, t_lat, d = x.shape
    t_ctx = ctx.shape[1]
    depth = w_mod.shape[0]
    assert batch == SUBLANES and d == 1024
    q_rank = mla_qn_g.shape[1]
    kv_rank = mla_kvn_g.shape[1]
    hidden = w_ffn_out.shape[1]
    th = hidden
    head_dim = MLA_NOPE + MLA_ROPE

    xl = x.reshape(batch * t_lat, d)
    xc = ctx.reshape(batch * t_ctx, d)

    cc = jnp.concatenate([c, c_ctx[None, :], jnp.zeros((BF16_ROWS - batch - 1, d), F32)], axis=0)
    mod = _mod_call(cc, w_mod, b_mod)

    mla_pair_lane = lambda i: MLA_NOPE + 2 * i
    diff_pair_lane = lambda i: 2 * i
    tabs = (_rope_lane_tables(t_lat, MLA_ROPE, mla_pair_lane, (0,))
            + _rope_lane_tables(t_lat, DIFF_HD, diff_pair_lane, (0, DIFF_HD)))
    perms = (_pair_swap_matrix(MLA_ROPE, mla_pair_lane, (0,)),
             _pair_swap_matrix(DIFF_HD, diff_pair_lane, (0, DIFF_HD)))
    ones_np = np.zeros((LANES, LANES), np.float32)
    ones_np[:DIFF_HD, :DIFF_HD] = 1.0
    ones_np[DIFF_HD:, DIFF_HD:] = 1.0
    ones_blk = jnp.asarray(ones_np, BF16)

    offs = np.cumsum([0, d, d, q_rank, kv_rank, MLA_ROPE, d, d, d, N_BRANCH * d])
    o_rx, o_rg, o_cq, o_ckv, o_kr, o_dq, o_dk, o_dv, o_mg = offs[:9]
    blk_w = d // RNN_BLOCKS

    for l in range(depth):
        need_ctx = l < depth - 1
        lam_init = 0.8 - 0.6 * math.exp(-0.3 * l)
        wl = w_in[l]
        mla_cols = jnp.concatenate(
            [wl[:, o_cq:o_cq + q_rank], wl[:, o_ckv:o_ckv + kv_rank],
             jnp.zeros((d, MLA_NOPE), F32), wl[:, o_kr:o_kr + MLA_ROPE],
             jnp.zeros((d, d - q_rank - kv_rank - head_dim), F32)], axis=1)
        w_in_p = jnp.concatenate(
            [mla_cols, wl[:, o_rx:o_rx + 2 * d], wl[:, o_dq:o_dq + (3 + N_BRANCH) * d]],
            axis=1).astype(BF16)
        w_uq_p = jnp.pad(mla_w_uq[l].reshape(q_rank, MLA_HEADS, head_dim),
                         ((0, 0), (0, 0), (0, LANES - head_dim))).reshape(q_rank, MLA_HEADS * LANES).astype(BF16)
        w_ukv = mla_w_ukv[l].reshape(kv_rank, MLA_HEADS, MLA_NOPE + MLA_V)
        w_k_p = jnp.pad(w_ukv[:, :, :MLA_NOPE],
                        ((0, 0), (0, 0), (0, LANES - MLA_NOPE))).reshape(kv_rank, MLA_HEADS * LANES).astype(BF16)
        w_v_p = jnp.pad(w_ukv[:, :, MLA_NOPE:],
                        ((0, 0), (0, 0), (0, LANES - MLA_V))).reshape(kv_rank, MLA_HEADS * LANES).astype(BF16)
        pad_gain = lambda g: jnp.pad(g, (0, LANES - head_dim)).reshape(1, LANES)
        gains = [mla_qn_g[l].reshape(1, q_rank), mla_kvn_g[l].reshape(1, kv_rank),
                 pad_gain(mla_q_g[l]), pad_gain(mla_k_g[l]),
                 jnp.tile(diff_q_g[l], 2).reshape(1, LANES), jnp.tile(diff_k_g[l], 2).reshape(1, LANES)]
        prep_w = (w_uq_p, w_k_p, w_v_p)
        w_ffn_in_p = w_ffn_in[l].astype(BF16)
        w_ffn_out_p = w_ffn_out[l].astype(BF16)
        lru_params = (conv_w[l], conv_b[l],
                      lru_wa[l].reshape(2 * RNN_BLOCKS, blk_w, blk_w).astype(BF16),
                      lru_wi[l].reshape(2 * RNN_BLOCKS, blk_w, blk_w).astype(BF16),
                      lru_ba[l], lru_bi[l], lru_lambda[l])
        merge_w = (diff_subln_g[l].reshape(1, DIFF_V), w_br_a[l].astype(BF16), w_br_b[l].astype(BF16),
                   w_br_c[l].astype(BF16), w_out[l].astype(BF16))

        mod6 = mod[l].reshape(mod.shape[1], 6, d)
        lat_mod = [mod6[:batch, i].reshape(batch, 1, d) for i in range(6)]
        ctx_mod = [mod6[batch:batch + 1, i].reshape(1, 1, d) for i in range(6)]

        yc = _inproj_call(xc, norm1_g[l], ctx_mod[1], ctx_mod[0], w_in_p)
        yc3 = yc.reshape(batch, t_ctx, yc.shape[1])
        yfc, ybc, h_ctx = _lru_call(yc3, jnp.zeros((2, batch, d), F32), *lru_params)
        qmc, kmc, vmc, qdc, kdc, vdc = _prep_call(yc, (), gains, prep_w, (ones_blk,), t_ctx)

        yl = _inproj_call(xl, norm1_g[l], lat_mod[1], lat_mod[0], w_in_p)
        yl3 = yl.reshape(batch, t_lat, yl.shape[1])
        yfl, ybl, _ = _lru_call(yl3, h_ctx, *lru_params)
        qml, kml, vml, qdl, kdl, vdl = _prep_call(yl, tabs, gains, prep_w, (ones_blk,) + perms, t_lat)

        mla_hps, diff_hps = 8, 4
        mla_kw = dict(batch=batch, heads_per_step=mla_hps, qw=LANES, vw=LANES, ow=MLA_V, n_heads=MLA_HEADS,
                      n_pbuf=2)
        diff_kw = dict(batch=batch, heads_per_step=diff_hps, qw=LANES, vw=2 * DIFF_V, ow=DIFF_V,
                       n_heads=DIFF_HEADS, n_pbuf=2)
        diff_kernel = functools.partial(_diff_attn_kernel, lam_init=lam_init)
        tq = 512
        o_b = _attn_call(_mla_attn_kernel, (), qml, [(kml, vml, t_lat), (kmc, vmc, t_ctx)],
                         t_q=t_lat, tq=tq, vcol0s=(0, 0), name="mla_attention", **mla_kw)
        o_c = _attn_call(diff_kernel, (diff_lambda[l],), qdl, [(kdl, vdl, t_lat), (kdc, vdc, t_ctx)],
                         t_q=t_lat, tq=tq, vcol0s=(0, 0), name="diff_attention", **diff_kw)
        xl_new = _merge_call(xl, yfl.reshape(-1, d), ybl.reshape(-1, d), yl, o_b, o_c, lat_mod[2],
                             *merge_w, lam_init)
        xl = _ffn_call(xl_new, norm2_g[l], lat_mod[4], lat_mod[3], lat_mod[5], w_ffn_in_p, w_ffn_out_p, th)

        if need_ctx:
            o_bc = _attn_call(_mla_attn_kernel, (), qmc, [(kmc, vmc, t_ctx)],
                              t_q=t_ctx, tq=t_ctx, vcol0s=(0,), name="mla_attention_ctx", **mla_kw)
            o_cc = _attn_call(diff_kernel, (diff_lambda[l],), qdc, [(kdc, vdc, t_ctx)],
                              t_q=t_ctx, tq=t_ctx, vcol0s=(0,), name="diff_attention_ctx", **diff_kw)
            xc_new = _merge_call(xc, yfc.reshape(-1, d), ybc.reshape(-1, d), yc, o_bc, o_cc, ctx_mod[2],
                                 *merge_w, lam_init)
            xc = _ffn_call(xc_new, norm2_g[l], ctx_mod[4], ctx_mod[3], ctx_mod[5], w_ffn_in_p, w_ffn_out_p, th)

    return xl.reshape(batch, t_lat, d)
```

```python
import functools
import math

import numpy as np
import jax
import jax.numpy as jnp
from jax import lax
from jax.experimental import pallas as pl
from jax.experimental.pallas import tpu as pltpu

F32 = jnp.float32
BF16 = jnp.bfloat16

EPS = 1e-6
GRID_W = 64
ROPE_BASE = 10000.0
LRU_C = 8.0
CONV_W = 4
RNN_BLOCKS = 8
MLA_HEADS = 16
MLA_NOPE = 64
MLA_ROPE = 32
MLA_V = 64
MLA_SCALE = (MLA_NOPE + MLA_ROPE) ** -0.5
DIFF_HEADS = 8
DIFF_HD = 64
DIFF_V = 2 * DIFF_HD
DIFF_SCALE = DIFF_HD ** -0.5
N_BRANCH = 3
LOG2E = math.log2(math.e)

LANES = 128
SUBLANES = 8
BF16_ROWS = 16
VMEM_LIMIT = 56 * 1024 * 1024

COL_MLA, COL_RX, COL_RG, COL_DQ, COL_DK, COL_DV, COL_MG = 0, 1, 2, 3, 4, 5, 6


def _cparams(n_axes, flags=None):
    return pltpu.CompilerParams(dimension_semantics=("arbitrary",) * n_axes,
                                vmem_limit_bytes=VMEM_LIMIT, flags=flags)


def _full(a, n_axes):
    return pl.BlockSpec(a.shape, lambda *_: (0,) * a.ndim)


def _sigmoid(x):
    return 0.5 * jnp.tanh(0.5 * x) + 0.5


def _silu(x):
    return x * _sigmoid(x)


def _gelu_tanh(x):
    return 0.5 * x * (1.0 + jnp.tanh(math.sqrt(2.0 / math.pi) * (x + 0.044715 * (x * x * x))))


def _rms(x, denom):
    ms = jnp.sum(x * x, axis=-1, keepdims=True) * (1.0 / denom)
    return x * lax.rsqrt(ms + EPS)


def _mod_kernel(c_ref, w_ref, b_ref, o_ref):
    s = _silu(c_ref[...]).astype(BF16)
    o_ref[...] = jnp.dot(s, w_ref[...].astype(BF16), preferred_element_type=F32) + b_ref[...]


def _mod_call(cc, w_mod, b_mod):
    depth, d, n = w_mod.shape
    tn = 1024
    return pl.pallas_call(
        _mod_kernel,
        grid=(depth, n // tn),
        in_specs=[pl.BlockSpec((cc.shape[0], d), lambda l, j: (0, 0)),
                  pl.BlockSpec((None, d, tn), lambda l, j: (l, 0, j)),
                  pl.BlockSpec((None, 1, tn), lambda l, j: (l, 0, j))],
        out_specs=pl.BlockSpec((None, cc.shape[0], tn), lambda l, j: (l, 0, j)),
        out_shape=jax.ShapeDtypeStruct((depth, cc.shape[0], n), F32),
        compiler_params=_cparams(2),
        name="adaln_mod",
    )(cc, w_mod, b_mod.reshape(depth, 1, n))


def _inproj_kernel(x_ref, g_ref, sc_ref, sh_ref, w_ref, o_ref, h_scr):
    @pl.when(pl.program_id(1) == 0)
    def _():
        xn = _rms(x_ref[...], x_ref.shape[-1]) * g_ref[...]
        h_scr[...] = (xn * (1.0 + sc_ref[...]) + sh_ref[...]).astype(BF16)

    o_ref[...] = jnp.dot(h_scr[...], w_ref[...], preferred_element_type=F32).astype(o_ref.dtype)


def _inproj_call(xs, norm_g, sc, sh, w_in_p):
    rows, d = xs.shape
    n = w_in_p.shape[1]
    tm, tn = min(1024, rows // sc.shape[0]), 2304
    per_seq = rows // tm // sc.shape[0]
    seq = lambda i, j: (i // per_seq, 0, 0)
    return pl.pallas_call(
        _inproj_kernel,
        grid=(rows // tm, n // tn),
        in_specs=[pl.BlockSpec((tm, d), lambda i, j: (i, 0)),
                  pl.BlockSpec((1, d), lambda i, j: (0, 0)),
                  pl.BlockSpec((None, 1, d), seq),
                  pl.BlockSpec((None, 1, d), seq),
                  pl.BlockSpec((d, tn), lambda i, j: (0, j))],
        out_specs=pl.BlockSpec((tm, tn), lambda i, j: (i, j)),
        out_shape=jax.ShapeDtypeStruct((rows, n), BF16),
        scratch_shapes=[pltpu.VMEM((tm, d), BF16)],
        compiler_params=_cparams(2),
        name="norm_inproj",
    )(xs, norm_g.reshape(1, d), sc, sh, w_in_p)


def _prep_kernel(*refs, use_rope):
    if use_rope:
        (ymla_ref, ydq_ref, ydk_ref, ydv_ref, cm_ref, sm_ref, cd_ref, sd_ref,
         qn_g_ref, kvn_g_ref, q_g_ref, k_g_ref, dq_g_ref, dk_g_ref,
         wuq_ref, wk_ref, wv_ref, ones_ref, pm_ref, pd_ref,
         qm_ref, km_ref, vm_ref, qd_ref, kd_ref, vd_ref) = refs
        cm, sm, cd, sd = cm_ref[...], sm_ref[...], cd_ref[...], sd_ref[...]
        pm, pd = pm_ref[...], pd_ref[...]
    else:
        (ymla_ref, ydq_ref, ydk_ref, ydv_ref,
         qn_g_ref, kvn_g_ref, q_g_ref, k_g_ref, dq_g_ref, dk_g_ref,
         wuq_ref, wk_ref, wv_ref, ones_ref,
         qm_ref, km_ref, vm_ref, qd_ref, kd_ref, vd_ref) = refs
        cm = sm = cd = sd = pm = pd = None
    q_rank = qn_g_ref.shape[-1]
    kv_rank = kvn_g_ref.shape[-1]
    ymla = ymla_ref[...].astype(F32)
    cq = ymla[:, :q_rank]
    ckv = ymla[:, q_rank:q_rank + kv_rank]
    kr = ymla[:, q_rank + kv_rank:q_rank + kv_rank + LANES]
    head_dim = MLA_NOPE + MLA_ROPE

    def rope(u, cos, sin, perm):
        if not use_rope:
            return u
        return u * cos + jnp.dot(u.astype(BF16), perm, preferred_element_type=F32) * sin

    cqn = (_rms(cq, q_rank) * qn_g_ref[...]).astype(BF16)
    q = jnp.dot(cqn, wuq_ref[...], preferred_element_type=F32)
    q_gain = q_g_ref[...] * (MLA_SCALE * LOG2E)
    for h in range(MLA_HEADS):
        u = _rms(q[:, h * LANES:(h + 1) * LANES], head_dim) * q_gain
        qm_ref[:, h * LANES:(h + 1) * LANES] = rope(u, cm, sm, pm).astype(BF16)

    ckvn = (_rms(ckv, kv_rank) * kvn_g_ref[...]).astype(BF16)
    vm = jnp.dot(ckvn, wv_ref[...], preferred_element_type=F32)
    sum_lane = (lax.broadcasted_iota(jnp.int32, vm.shape, 1) & (LANES - 1)) == MLA_V
    vm_ref[...] = jnp.where(sum_lane, 1.0, vm).astype(BF16)
    kn = jnp.dot(ckvn, wk_ref[...], preferred_element_type=F32)
    k_gain = k_g_ref[...]
    for h in range(MLA_HEADS):
        u = _rms(kn[:, h * LANES:(h + 1) * LANES] + kr, head_dim) * k_gain
        km_ref[:, h * LANES:(h + 1) * LANES] = rope(u, cm, sm, pm).astype(BF16)

    ones_blk = ones_ref[...]

    def diff_norm_rope(y_ref, gain, o_ref):
        for h in range(DIFF_HEADS):
            x = y_ref[:, h * LANES:(h + 1) * LANES].astype(F32)
            ms = jnp.dot((x * x).astype(BF16), ones_blk, preferred_element_type=F32) * (1.0 / DIFF_HD)
            u = x * lax.rsqrt(ms + EPS) * gain
            o_ref[:, h * LANES:(h + 1) * LANES] = rope(u, cd, sd, pd).astype(BF16)

    diff_norm_rope(ydq_ref, dq_g_ref[...] * (DIFF_SCALE * LOG2E), qd_ref)
    diff_norm_rope(ydk_ref, dk_g_ref[...], kd_ref)

    tm = ydv_ref.shape[0]
    one_hot = (lax.broadcasted_iota(jnp.int32, (tm, LANES), 1) == 0).astype(BF16)
    for h in range(DIFF_HEADS):
        vd_ref[:, 2 * h * LANES:(2 * h + 1) * LANES] = ydv_ref[:, h * LANES:(h + 1) * LANES]
        vd_ref[:, (2 * h + 1) * LANES:(2 * h + 2) * LANES] = one_hot


def _prep_call(y, tabs, gains, weights, consts, t_seq):
    rows = y.shape[0]
    d = 1024
    tm = 256
    use_rope = len(tabs) > 0
    per_seq = t_seq // tm
    row_blk = lambda w: pl.BlockSpec((tm, w), lambda i: (i, 0))
    col_blk = lambda c: pl.BlockSpec((tm, d), lambda i, c=c: (i, c))
    tab_blk = pl.BlockSpec((tm, LANES), lambda i: (i % per_seq, 0))
    wq = MLA_HEADS * LANES
    out_shapes = [jax.ShapeDtypeStruct((rows, wq), BF16), jax.ShapeDtypeStruct((rows, wq), BF16),
                  jax.ShapeDtypeStruct((rows, wq), BF16),
                  jax.ShapeDtypeStruct((rows, d), BF16), jax.ShapeDtypeStruct((rows, d), BF16),
                  jax.ShapeDtypeStruct((rows, 2 * d), BF16)]
    small = list(gains) + list(weights) + list(consts)
    return pl.pallas_call(
        functools.partial(_prep_kernel, use_rope=use_rope),
        grid=(rows // tm,),
        in_specs=[col_blk(COL_MLA), col_blk(COL_DQ), col_blk(COL_DK), col_blk(COL_DV)]
                 + [tab_blk] * len(tabs) + [_full(a, 1) for a in small],
        out_specs=[row_blk(s.shape[1]) for s in out_shapes],
        out_shape=out_shapes,
        compiler_params=_cparams(1),
        name="attn_prep_rope" if use_rope else "attn_prep",
    )(y, y, y, y, *tabs, *small)


def _lru_kernel(pf_ref, cf_ref, nf_ref, pb_ref, cb_ref, nb_ref, h0_ref,
                cw_ref, cbias_ref, wa_ref, wi_ref, ba_ref, bi_ref, lam_ref,
                yf_ref, yb_ref, hfin_ref, h_scr, win_scr, a_scr, b_scr, y_scr):
    k = pl.program_id(0)
    nk = pl.num_programs(0)
    batch, tt, w = cf_ref.shape
    halo = pf_ref.shape[1]
    tm = tt * batch
    blk_w = w // RNN_BLOCKS

    @pl.when(k == 0)
    def _():
        h_scr[...] = h0_ref[...]

    def interleave(src_ref, row0, n_steps, keep):
        for b in range(batch):
            v = src_ref[b].astype(F32)
            if keep is not None:
                v = jnp.where(keep, v, 0.0)
            for n in range(RNN_BLOCKS):
                win_scr[n, pl.ds(row0 + b, n_steps, stride=batch), :] = v[:, n * blk_w:(n + 1) * blk_w]

    def direction(r, prev_ref, cur_ref, next_ref, has_prev, has_next, y_ref):
        interleave(prev_ref, 0, halo, has_prev)
        interleave(cur_ref, halo * batch, tt, None)
        interleave(next_ref, (halo + tt) * batch, halo, has_next)
        base = halo * batch
        for n in range(RNN_BLOCKS):
            lanes = slice(n * blk_w, (n + 1) * blk_w)
            xc = cbias_ref[:, lanes]
            for j in range(CONV_W):
                off = base + (j - CONV_W // 2) * batch
                xc = xc + win_scr[n, off:off + tm, :] * cw_ref[j:j + 1, lanes]
            xb = xc.astype(BF16)
            gr = jnp.dot(xb, wa_ref[r * RNN_BLOCKS + n], preferred_element_type=F32) + ba_ref[r][:, lanes]
            gi = jnp.dot(xb, wi_ref[r * RNN_BLOCKS + n], preferred_element_type=F32) + bi_ref[r][:, lanes]
            lam = lam_ref[r][:, lanes]
            softplus_neg = jnp.maximum(-lam, 0.0) + jnp.log(1.0 + jnp.exp(-jnp.abs(lam)))
            a = jnp.exp2(_sigmoid(gr) * (softplus_neg * (-LRU_C * LOG2E)))
            a_scr[n] = a
            gap = 1.0 - a * a
            b_scr[n] = gap * lax.rsqrt(jnp.maximum(gap, 1e-30)) * _sigmoid(gi) * xc
        h = [h_scr[r, :, n * blk_w:(n + 1) * blk_w] for n in range(RNN_BLOCKS)]
        order = range(tt) if r == 0 else range(tt - 1, -1, -1)
        for j in order:
            rows = pl.ds(j * batch, batch)
            for n in range(RNN_BLOCKS):
                h[n] = a_scr[n, rows, :] * h[n] + b_scr[n, rows, :]
                y_scr[n, rows, :] = h[n]
        for n in range(RNN_BLOCKS):
            h_scr[r, :, n * blk_w:(n + 1) * blk_w] = h[n]
        for b in range(batch):
            for n in range(RNN_BLOCKS):
                y_ref[b, :, n * blk_w:(n + 1) * blk_w] = (
                    y_scr[n, pl.ds(b, tt, stride=batch), :].astype(y_ref.dtype))

    direction(0, pf_ref, cf_ref, nf_ref, k > 0, k < nk - 1, yf_ref)
    direction(1, pb_ref, cb_ref, nb_ref, k < nk - 1, k > 0, yb_ref)

    @pl.when(k == nk - 1)
    def _():
        hfin_ref[...] = h_scr[...]


def _lru_call(y3, h0, conv_w, conv_b, wa, wi, ba, bi, lam):
    batch, t_seq, _ = y3.shape
    w = 1024
    tt = 64
    halo = BF16_ROWS
    per = tt // halo
    nk = t_seq // tt
    last = t_seq // halo - 1

    def specs(chunk):
        return [pl.BlockSpec((batch, halo, w), lambda s: (0, jnp.maximum(chunk(s) * per - 1, 0), COL_RX)),
                pl.BlockSpec((batch, tt, w), lambda s: (0, chunk(s), COL_RX)),
                pl.BlockSpec((batch, halo, w), lambda s: (0, jnp.minimum((chunk(s) + 1) * per, last), COL_RX))]

    fwd = lambda s: s
    bwd = lambda s: nk - 1 - s
    params = [conv_w, conv_b.reshape(1, w), wa, wi, ba.reshape(2, 1, w), bi.reshape(2, 1, w),
              lam.reshape(2, 1, w)]
    tm = tt * batch
    return pl.pallas_call(
        _lru_kernel,
        grid=(nk,),
        in_specs=specs(fwd) + specs(bwd) + [_full(h0, 1)] + [_full(a, 1) for a in params],
        out_specs=[pl.BlockSpec((batch, tt, w), lambda s: (0, fwd(s), 0)),
                   pl.BlockSpec((batch, tt, w), lambda s: (0, bwd(s), 0)),
                   pl.BlockSpec((2, batch, w), lambda s: (0, 0, 0))],
        out_shape=[jax.ShapeDtypeStruct((batch, t_seq, w), BF16)] * 2
                  + [jax.ShapeDtypeStruct((2, batch, w), F32)],
        scratch_shapes=[pltpu.VMEM((2, batch, w), F32),
                        pltpu.VMEM((RNN_BLOCKS, (tt + 2 * halo) * batch, w // RNN_BLOCKS), F32)]
                       + [pltpu.VMEM((RNN_BLOCKS, tm, w // RNN_BLOCKS), F32)] * 3,
        compiler_params=_cparams(1),
        name="conv_rglru_scan",
    )(y3, y3, y3, y3, y3, y3, h0, *params)


KEY_CHUNK = 512
ROW_STRIP = 64


def _key_chunks(kv_pieces):
    chunks = []
    col = 0
    for k_ref, v_ref in kv_pieces:
        n = k_ref.shape[0]
        kc = min(KEY_CHUNK, n)
        for r0 in range(0, n, kc):
            chunks.append((k_ref, v_ref, r0, col, kc))
            col += kc
    return chunks


def _lane_tiles(c0, kc):
    return [slice(c0 + i * LANES, c0 + (i + 1) * LANES) for i in range(kc // LANES)]


def _row_strips(tq):
    return [slice(r * ROW_STRIP, (r + 1) * ROW_STRIP) for r in range(tq // ROW_STRIP)]


def _pv_chunk(p_scr, chunk, vcols):
    _, v_ref, r0, c0, kc = chunk
    return jnp.dot(p_scr[:, c0:c0 + kc], v_ref[r0:r0 + kc, vcols], preferred_element_type=F32)


def _score_exp(q, chunks, cols, s_scr, p_scr, want_sum):
    tq = q.shape[0]
    for k_ref, _, r0, c0, kc in chunks:
        s_scr[:, c0:c0 + kc] = lax.dot_general(q, k_ref[r0:r0 + kc, cols], (((1,), (1,)), ((), ())),
                                               preferred_element_type=F32)
    tiles = [lanes for chunk in chunks for lanes in _lane_tiles(chunk[3], chunk[4])]
    sums = []
    for rows in _row_strips(tq):
        m = functools.reduce(jnp.maximum, [s_scr[rows, lanes] for lanes in tiles])
        m = jnp.broadcast_to(jnp.max(m, axis=-1, keepdims=True), (ROW_STRIP, LANES))
        lpart = None
        for lanes in tiles:
            e = jnp.exp2(s_scr[rows, lanes] - m)
            if want_sum:
                lpart = e if lpart is None else lpart + e
            p_scr[rows, lanes] = e.astype(BF16)
        if want_sum:
            sums.append(jnp.sum(lpart, axis=-1, keepdims=True))
    return jnp.concatenate(sums, axis=0) if want_sum else None


def _softmax_pv(q, kv_pieces, cols, vcols, s_scr, p_scr, sum_lane=None):
    chunks = _key_chunks(kv_pieces)
    l = _score_exp(q, chunks, cols, s_scr, p_scr, want_sum=sum_lane is None)
    acc = functools.reduce(jnp.add, [_pv_chunk(p_scr, chunk, vcols) for chunk in chunks])
    if sum_lane is not None:
        l = acc[:, sum_lane:sum_lane + 1]
    return acc * (1.0 / l)


class _UnitRunner:
    def __init__(self, n_units, tq, kv, s_scr, p_scr, sum_lane=None):
        self.n_units, self.tq, self.kv, self.s_scr, self.p_scr = n_units, tq, kv, s_scr, p_scr
        self.sum_lane = sum_lane
        self.done = 0
        self.pieces = 0

    def run(self, q_of_rows, cols, vcols):
        edge = self.done in (0, self.n_units - 1)
        self.done += 1
        half = self.tq // 2
        if edge and half % ROW_STRIP == 0:
            row_sets = [pl.ds(0, half), pl.ds(half, half)]
        else:
            row_sets = [pl.ds(0, self.tq)]
        outs = []
        for rows in row_sets:
            buf = self.pieces % 2
            self.pieces += 1
            outs.append(_softmax_pv(q_of_rows(rows), self.kv, cols, vcols,
                                    self.s_scr.at[buf, rows], self.p_scr.at[buf, rows], self.sum_lane))
        return outs[0] if len(outs) == 1 else jnp.concatenate(outs, axis=0)


def _mla_attn_kernel(q_ref, *refs):
    o_ref, s_scr, p_scr = refs[-3:]
    kv = [(refs[i], refs[i + 1]) for i in range(0, len(refs) - 3, 2)]
    tq = q_ref.shape[0]
    n_heads = q_ref.shape[1] // LANES
    lane = lax.broadcasted_iota(jnp.int32, (tq, LANES), 1)
    units = _UnitRunner(n_heads, tq, kv, s_scr, p_scr, sum_lane=MLA_V)
    for p in range(n_heads // 2):
        outs = []
        for hh in (2 * p, 2 * p + 1):
            cols = slice(hh * LANES, (hh + 1) * LANES)
            outs.append(units.run(lambda rows, cols=cols: q_ref[rows, cols], cols, cols))
        pair = jnp.where(lane < MLA_V, outs[0], pltpu.roll(outs[1], MLA_V, axis=1))
        o_ref[:, p * LANES:(p + 1) * LANES] = pair.astype(o_ref.dtype)


def _diff_attn_kernel(dl_ref, q_ref, *refs, lam_init):
    o_ref, s_scr, p_scr = refs[-3:]
    kv = [(refs[i], refs[i + 1]) for i in range(0, len(refs) - 3, 2)]
    dl = dl_ref[...]
    lam = (jnp.exp(jnp.sum(dl[0:1] * dl[1:2], axis=-1, keepdims=True))
           - jnp.exp(jnp.sum(dl[2:3] * dl[3:4], axis=-1, keepdims=True)) + lam_init)
    n_heads = q_ref.shape[1] // LANES
    units = _UnitRunner(2 * n_heads, q_ref.shape[0], kv, s_scr, p_scr, sum_lane=DIFF_V)
    for h in range(n_heads):
        cols = slice(h * LANES, (h + 1) * LANES)
        vcols = slice(2 * h * LANES, (2 * h + 2) * LANES)

        def map_q(rows, first, cols=cols):
            q = q_ref[rows, cols].astype(F32)
            in_first = lax.broadcasted_iota(jnp.int32, q.shape, 1) < DIFF_HD
            return (jnp.where(in_first, q, 0.0) if first else jnp.where(in_first, 0.0, q)).astype(BF16)

        o0 = units.run(functools.partial(map_q, first=True), cols, vcols)[:, :DIFF_V]
        o1 = units.run(functools.partial(map_q, first=False), cols, vcols)[:, :DIFF_V]
        o_ref[:, cols] = (o0 - lam * o1).astype(o_ref.dtype)


def _attn_call(kernel, extra, q, kv_arrays, *, batch, t_q, tq, heads_per_step, qw, vw, ow, n_heads, vcol0s,
               n_pbuf, name):
    groups = n_heads // heads_per_step
    nq = t_q // tq
    n_keys = sum(t_kv for _, _, t_kv in kv_arrays)
    kv_specs, kv_args = [], []
    for (k, v, t_kv), vcol0 in zip(kv_arrays, vcol0s):
        kv_specs += [pl.BlockSpec((t_kv, heads_per_step * qw), lambda b, g, i: (b, g)),
                     pl.BlockSpec((t_kv, heads_per_step * vw), lambda b, g, i, c=vcol0: (b, c + g))]
        kv_args += [k, v]
    return pl.pallas_call(
        kernel,
        grid=(batch, groups, nq),
        in_specs=[_full(a, 3) for a in extra]
                 + [pl.BlockSpec((tq, heads_per_step * qw), lambda b, g, i: (b * nq + i, g))] + kv_specs,
        out_specs=pl.BlockSpec((tq, heads_per_step * ow), lambda b, g, i: (b * nq + i, g)),
        out_shape=jax.ShapeDtypeStruct((batch * t_q, n_heads * ow), BF16),
        scratch_shapes=[pltpu.VMEM((2, tq, n_keys), F32), pltpu.VMEM((n_pbuf, tq, n_keys), BF16)],
        compiler_params=_cparams(3),
        name=name,
    )(*extra, q, *kv_args)


def _merge_kernel(x_ref, yf_ref, yb_ref, rg_ref, ob_ref, oc_ref, mg_ref, g1_ref, sub_g_ref,
                  wa_ref, wb_ref, wc_ref, wo_ref, o_ref, *, lam_init):
    d = x_ref.shape[-1]
    ya = yf_ref[...].astype(F32) + yb_ref[...].astype(F32)
    za = (ya * _gelu_tanh(rg_ref[...].astype(F32))).astype(BF16)
    br_a = jnp.dot(za, wa_ref[...], preferred_element_type=F32)
    br_b = jnp.dot(ob_ref[...], wb_ref[...], preferred_element_type=F32)
    oc = oc_ref[...].astype(F32)
    sub_gain = sub_g_ref[...] * (1.0 - lam_init)
    oc_n = jnp.concatenate(
        [_rms(oc[:, h * DIFF_V:(h + 1) * DIFF_V], DIFF_V) * sub_gain for h in range(DIFF_HEADS)], axis=1)
    br_c = jnp.dot(oc_n.astype(BF16), wc_ref[...], preferred_element_type=F32)
    mg = mg_ref[...].astype(F32)
    mix = (_sigmoid(mg[:, :d]) * br_a + _sigmoid(mg[:, d:2 * d]) * br_b
           + _sigmoid(mg[:, 2 * d:]) * br_c)
    m = jnp.dot(mix.astype(BF16), wo_ref[...], preferred_element_type=F32)
    o_ref[...] = x_ref[...] + g1_ref[...] * m


def _merge_call(xs, yf, yb, y, ob, oc, g1, sub_g, wa, wb, wc, wo, lam_init):
    rows, d = xs.shape
    tm = 512
    per_seq = rows // tm // g1.shape[0]
    blk = lambda: pl.BlockSpec((tm, d), lambda i: (i, 0))
    return pl.pallas_call(
        functools.partial(_merge_kernel, lam_init=lam_init),
        grid=(rows // tm,),
        in_specs=[blk(), blk(), blk(),
                  pl.BlockSpec((tm, d), lambda i: (i, COL_RG)),
                  blk(), blk(),
                  pl.BlockSpec((tm, N_BRANCH * d), lambda i: (i, COL_MG // N_BRANCH)),
                  pl.BlockSpec((None, 1, d), lambda i: (i // per_seq, 0, 0)),
                  _full(sub_g, 1), _full(wa, 1), _full(wb, 1), _full(wc, 1), _full(wo, 1)],
        out_specs=blk(),
        out_shape=jax.ShapeDtypeStruct((rows, d), F32),
        compiler_params=_cparams(1),
        name="branch_merge",
    )(xs, yf, yb, y, ob, oc, y, g1, sub_g, wa, wb, wc, wo)


def _ffn_kernel(x_ref, g_ref, sc_ref, sh_ref, g2_ref, wg_ref, wu_ref, wout_ref, o_ref, h_scr, acc_scr):
    j = pl.program_id(1)

    @pl.when(j == 0)
    def _():
        xn = _rms(x_ref[...], x_ref.shape[-1]) * g_ref[...]
        h_scr[...] = (xn * (1.0 + sc_ref[...]) + sh_ref[...]).astype(BF16)
        acc_scr[...] = jnp.zeros_like(acc_scr)

    h = h_scr[...]
    gate = jnp.dot(h, wg_ref[...], preferred_element_type=F32)
    up = jnp.dot(h, wu_ref[...], preferred_element_type=F32)
    act = (_silu(gate) * up).astype(BF16)
    acc_scr[...] += jnp.dot(act, wout_ref[...], preferred_element_type=F32)

    @pl.when(j == pl.num_programs(1) - 1)
    def _():
        o_ref[...] = x_ref[...] + g2_ref[...] * acc_scr[...]


def _ffn_call(xs, norm_g, sc, sh, g2, w_in, w_out, th):
    rows, d = xs.shape
    hidden = w_out.shape[0]
    nj = hidden // th
    tm = 512
    per_seq = rows // tm // sc.shape[0]
    seq = lambda i, j: (i // per_seq, 0, 0)
    wmode = dict(pipeline_mode=pl.Buffered(1)) if nj == 1 else {}
    return pl.pallas_call(
        _ffn_kernel,
        grid=(rows // tm, nj),
        in_specs=[pl.BlockSpec((tm, d), lambda i, j: (i, 0)),
                  pl.BlockSpec((1, d), lambda i, j: (0, 0)),
                  pl.BlockSpec((None, 1, d), seq),
                  pl.BlockSpec((None, 1, d), seq),
                  pl.BlockSpec((None, 1, d), seq),
                  pl.BlockSpec((d, th), lambda i, j: (0, j), **wmode),
                  pl.BlockSpec((d, th), lambda i, j: (0, nj + j), **wmode),
                  pl.BlockSpec((th, d), lambda i, j: (j, 0), **wmode)],
        out_specs=pl.BlockSpec((tm, d), lambda i, j: (i, 0)),
        out_shape=jax.ShapeDtypeStruct((rows, d), F32),
        scratch_shapes=[pltpu.VMEM((tm, d), BF16), pltpu.VMEM((tm, d), F32)],
        compiler_params=_cparams(2),
        name="norm_swiglu_ffn",
    )(xs, norm_g.reshape(1, d), sc, sh, g2, w_in, w_in, w_out)


def _rope_lane_tables(t_lat, rot_dim, lane_of_pair, copies):
    rows_n = t_lat // GRID_W
    row_ids = jnp.repeat(jnp.arange(rows_n, dtype=F32), GRID_W)
    col_ids = jnp.tile(jnp.arange(GRID_W, dtype=F32), rows_n)
    n = rot_dim // 4
    freqs = ROPE_BASE ** (-jnp.arange(n, dtype=F32) / n)
    ang = jnp.concatenate([row_ids[:, None] * freqs, col_ids[:, None] * freqs], axis=-1)
    cos, sin = jnp.cos(ang), jnp.sin(ang)
    npairs = rot_dim // 2
    sel_c = np.zeros((npairs, LANES), np.float32)
    sel_s = np.zeros((npairs, LANES), np.float32)
    base_c = np.ones((LANES,), np.float32)
    for off in copies:
        for i in range(npairs):
            lane = off + lane_of_pair(i)
            sel_c[i, lane] = sel_c[i, lane + 1] = 1.0
            sel_s[i, lane] = -1.0
            sel_s[i, lane + 1] = 1.0
            base_c[lane] = base_c[lane + 1] = 0.0
    pick_c = np.argmax(sel_c, axis=0)
    pick_s = np.argmax(np.abs(sel_s), axis=0)
    c_tab = jnp.where(jnp.asarray(base_c > 0)[None, :], 1.0, cos[:, pick_c])
    s_tab = sin[:, pick_s] * jnp.asarray(sel_s.sum(axis=0))[None, :]
    return c_tab, s_tab


def _pair_swap_matrix(rot_dim, lane_of_pair, copies):
    p = np.zeros((LANES, LANES), np.float32)
    for off in copies:
        for i in range(rot_dim // 2):
            lane = off + lane_of_pair(i)
            p[lane + 1, lane] = 1.0
            p[lane, lane + 1] = 1.0
    return jnp.asarray(p, BF16)


def kernel(x, c, ctx, c_ctx, w_mod, b_mod, norm1_g, norm2_g, w_in, conv_w, conv_b, lru_wa, lru_ba, lru_wi, lru_bi, lru_lambda, mla_qn_g, mla_w_uq, mla_kvn_g, mla_w_ukv, mla_q_g, mla_k_g, diff_q_g, diff_k_g, diff_lambda, diff_subln_g, w_br_a, w_br_b, w_br_c, w_out, w_ffn_in, w_ffn_out):
    batch, t_lat, d = x.shape
    t_ctx = ctx.shape[1]
    depth = w_mod.shape[0]
    assert batch == SUBLANES and d == 1024
    q_rank = mla_qn_g.shape[1]
    kv_rank = mla_kvn_g.shape[1]
    hidden = w_ffn_out.shape[1]
    th = hidden
    head_dim = MLA_NOPE + MLA_ROPE

    xl = x.reshape(batch * t_lat, d)
    xc = ctx.reshape(batch * t_ctx, d)

    cc = jnp.concatenate([c, c_ctx[None, :], jnp.zeros((BF16_ROWS - batch - 1, d), F32)], axis=0)
    mod = _mod_call(cc, w_mod, b_mod)

    mla_pair_lane = lambda i: MLA_NOPE + 2 * i
    diff_pair_lane = lambda i: 2 * i
    tabs = (_rope_lane_tables(t_lat, MLA_ROPE, mla_pair_lane, (0,))
            + _rope_lane_tables(t_lat, DIFF_HD, diff_pair_lane, (0, DIFF_HD)))
    perms = (_pair_swap_matrix(MLA_ROPE, mla_pair_lane, (0,)),
             _pair_swap_matrix(DIFF_HD, diff_pair_lane, (0, DIFF_HD)))
    ones_np = np.zeros((LANES, LANES), np.float32)
    ones_np[:DIFF_HD, :DIFF_HD] = 1.0
    ones_np[DIFF_HD:, DIFF_HD:] = 1.0
    ones_blk = jnp.asarray(ones_np, BF16)

    offs = np.cumsum([0, d, d, q_rank, kv_rank, MLA_ROPE, d, d, d, N_BRANCH * d])
    o_rx, o_rg, o_cq, o_ckv, o_kr, o_dq, o_dk, o_dv, o_mg = offs[:9]
    blk_w = d // RNN_BLOCKS

    for l in range(depth):
        need_ctx = l < depth - 1
        lam_init = 0.8 - 0.6 * math.exp(-0.3 * l)
        wl = w_in[l]
        mla_cols = jnp.concatenate(
            [wl[:, o_cq:o_cq + q_rank], wl[:, o_ckv:o_ckv + kv_rank],
             jnp.zeros((d, MLA_NOPE), F32), wl[:, o_kr:o_kr + MLA_ROPE],
             jnp.zeros((d, d - q_rank - kv_rank - head_dim), F32)], axis=1)
        w_in_p = jnp.concatenate(
            [mla_cols, wl[:, o_rx:o_rx + 2 * d], wl[:, o_dq:o_dq + (3 + N_BRANCH) * d]],
            axis=1).astype(BF16)
        w_uq_p = jnp.pad(mla_w_uq[l].reshape(q_rank, MLA_HEADS, head_dim),
                         ((0, 0), (0, 0), (0, LANES - head_dim))).reshape(q_rank, MLA_HEADS * LANES).astype(BF16)
        w_ukv = mla_w_ukv[l].reshape(kv_rank, MLA_HEADS, MLA_NOPE + MLA_V)
        w_k_p = jnp.pad(w_ukv[:, :, :MLA_NOPE],
                        ((0, 0), (0, 0), (0, LANES - MLA_NOPE))).reshape(kv_rank, MLA_HEADS * LANES).astype(BF16)
        w_v_p = jnp.pad(w_ukv[:, :, MLA_NOPE:],
                        ((0, 0), (0, 0), (0, LANES - MLA_V))).reshape(kv_rank, MLA_HEADS * LANES).astype(BF16)
        pad_gain = lambda g: jnp.pad(g, (0, LANES - head_dim)).reshape(1, LANES)
        gains = [mla_qn_g[l].reshape(1, q_rank), mla_kvn_g[l].reshape(1, kv_rank),
                 pad_gain(mla_q_g[l]), pad_gain(mla_k_g[l]),
                 jnp.tile(diff_q_g[l], 2).reshape(1, LANES), jnp.tile(diff_k_g[l], 2).reshape(1, LANES)]
        prep_w = (w_uq_p, w_k_p, w_v_p)
        w_ffn_in_p = w_ffn_in[l].astype(BF16)
        w_ffn_out_p = w_ffn_out[l].astype(BF16)
        lru_params = (conv_w[l], conv_b[l],
                      lru_wa[l].reshape(2 * RNN_BLOCKS, blk_w, blk_w).astype(BF16),
                      lru_wi[l].reshape(2 * RNN_BLOCKS, blk_w, blk_w).astype(BF16),
                      lru_ba[l], lru_bi[l], lru_lambda[l])
        merge_w = (diff_subln_g[l].reshape(1, DIFF_V), w_br_a[l].astype(BF16), w_br_b[l].astype(BF16),
                   w_br_c[l].astype(BF16), w_out[l].astype(BF16))

        mod6 = mod[l].reshape(mod.shape[1], 6, d)
        lat_mod = [mod6[:batch, i].reshape(batch, 1, d) for i in range(6)]
        ctx_mod = [mod6[batch:batch + 1, i].reshape(1, 1, d) for i in range(6)]

        yc = _inproj_call(xc, norm1_g[l], ctx_mod[1], ctx_mod[0], w_in_p)
        yc3 = yc.reshape(batch, t_ctx, yc.shape[1])
        yfc, ybc, h_ctx = _lru_call(yc3, jnp.zeros((2, batch, d), F32), *lru_params)
        qmc, kmc, vmc, qdc, kdc, vdc = _prep_call(yc, (), gains, prep_w, (ones_blk,), t_ctx)

        yl = _inproj_call(xl, norm1_g[l], lat_mod[1], lat_mod[0], w_in_p)
        yl3 = yl.reshape(batch, t_lat, yl.shape[1])
        yfl, ybl, _ = _lru_call(yl3, h_ctx, *lru_params)
        qml, kml, vml, qdl, kdl, vdl = _prep_call(yl, tabs, gains, prep_w, (ones_blk,) + perms, t_lat)

        mla_hps, diff_hps = 8, 8
        mla_kw = dict(batch=batch, heads_per_step=mla_hps, qw=LANES, vw=LANES, ow=MLA_V, n_heads=MLA_HEADS,
                      n_pbuf=2)
        diff_kw = dict(batch=batch, heads_per_step=diff_hps, qw=LANES, vw=2 * DIFF_V, ow=DIFF_V,
                       n_heads=DIFF_HEADS, n_pbuf=2)
        diff_kernel = functools.partial(_diff_attn_kernel, lam_init=lam_init)
        tq = 512
        o_b = _attn_call(_mla_attn_kernel, (), qml, [(kml, vml, t_lat), (kmc, vmc, t_ctx)],
                         t_q=t_lat, tq=tq, vcol0s=(0, 0), name="mla_attention", **mla_kw)
        o_c = _attn_call(diff_kernel, (diff_lambda[l],), qdl, [(kdl, vdl, t_lat), (kdc, vdc, t_ctx)],
                         t_q=t_lat, tq=tq, vcol0s=(0, 0), name="diff_attention", **diff_kw)
        xl_new = _merge_call(xl, yfl.reshape(-1, d), ybl.reshape(-1, d), yl, o_b, o_c, lat_mod[2],
                             *merge_w, lam_init)
        xl = _ffn_call(xl_new, norm2_g[l], lat_mod[4], lat_mod[3], lat_mod[5], w_ffn_in_p, w_ffn_out_p, th)

        if need_ctx:
            o_bc = _attn_call(_mla_attn_kernel, (), qmc, [(kmc, vmc, t_ctx)],
                              t_q=t_ctx, tq=t_ctx, vcol0s=(0,), name="mla_attention_ctx", **mla_kw)
            o_cc = _attn_call(diff_kernel, (diff_lambda[l],), qdc, [(kdc, vdc, t_ctx)],
                              t_q=t_ctx, tq=t_ctx, vcol0s=(0,), name="diff_attention_ctx", **diff_kw)
            xc_new = _merge_call(xc, yfc.reshape(-1, d), ybc.reshape(-1, d), yc, o_bc, o_cc, ctx_mod[2],
                                 *merge_w, lam_init)
            xc = _ffn_call(xc_new, norm2_g[l], ctx_mod[4], ctx_mod[3], ctx_mod[5], w_ffn_in_p, w_ffn_out_p, th)

    return xl.reshape(batch, t_lat, d)
```
